```python
import jax, jax.numpy as jnp
from jax import lax
import numpy as np

D_MODEL = 2048
BATCH = 8
SEQ = 4096
DEPTH = 1

CHUNK = 64
D_MIX = D_MODEL
D_CONV = D_MIX // 2
CONV_WIDTH = 31
D_MLSTM = D_MIX - D_CONV
MLSTM_HEADS = 4
DV = D_MLSTM // MLSTM_HEADS
DK = DV // 2
N_EXPERTS = 32
TOP_K = 4
D_EXPERT = D_MODEL
SWIGLU_ALPHA = 1.702
SWIGLU_LIMIT = 7.0
MOE_BLOCK = 256
LN_EPS = 1e-5
DEEPNORM_ALPHA = (2 * DEPTH) ** 0.25
DEEPNORM_BETA = (8 * DEPTH) ** -0.25
IN_SIZES = (D_CONV, D_CONV, MLSTM_HEADS * DK, MLSTM_HEADS * DK, D_MLSTM, D_MLSTM, MLSTM_HEADS, MLSTM_HEADS)
D_IN = sum(IN_SIZES)
SPLIT_POINTS = tuple(int(s) for s in np.cumsum(IN_SIZES)[:-1])

kernel_name = 'hymba_conformer_mlstm_moe_deepnorm'


def layer_norm(x, gain=None, bias=None):
    xf = x.astype(jnp.float32)
    mu = jnp.mean(xf, axis=-1, keepdims=True)
    var = jnp.mean(jnp.square(xf - mu), axis=-1, keepdims=True)
    y = (xf - mu) * lax.rsqrt(var + LN_EPS)
    if gain is not None:
        y = y * gain.astype(jnp.float32) + bias.astype(jnp.float32)
    return y.astype(x.dtype)


def conformer_conv(val, gate, dw_w, dw_b, ln_g, ln_b):
    u = val * jax.nn.sigmoid(gate)
    y = lax.conv_general_dilated(
        u, dw_w[:, None, :].astype(u.dtype), window_strides=(1,),
        padding=[(CONV_WIDTH - 1, 0)],
        dimension_numbers=('NWC', 'WIO', 'NWC'), feature_group_count=D_CONV)
    y = y + dw_b
    return jax.nn.silu(layer_norm(y, ln_g, ln_b))


def mlstm_chunkwise(q, k, v, i_pre, f_pre):
    bsz, seq = q.shape[:2]
    n_chunks = seq // CHUNK

    def to_chunks(z):
        z = z.astype(jnp.float32).reshape(bsz, n_chunks, CHUNK, MLSTM_HEADS, -1)
        return z.transpose(1, 0, 3, 2, 4)

    qc = to_chunks(q)
    kc = to_chunks(k) * (DK ** -0.5)
    vc = to_chunks(v)
    ic = to_chunks(i_pre[..., None])[..., 0]
    lfc = jax.nn.log_sigmoid(to_chunks(f_pre[..., None])[..., 0])
    causal = jnp.tril(jnp.ones((CHUNK, CHUNK), dtype=bool))

    def step(carry, inp):
        c_mat, n_vec, m = carry
        q_, k_, v_, ig, lf = inp
        b = jnp.cumsum(lf, axis=-1)
        dmat = jnp.where(causal, b[..., :, None] - b[..., None, :] + ig[..., None, :], -jnp.inf)
        inter = b + m[..., None]
        m_q = jnp.maximum(inter, jnp.max(dmat, axis=-1))
        s = jnp.einsum('bhld,bhsd->bhls', q_, k_) * jnp.exp(dmat - m_q[..., None])
        w_inter = jnp.exp(inter - m_q)
        num = jnp.einsum('bhls,bhsv->bhlv', s, v_) + w_inter[..., None] * jnp.einsum('bhld,bhdv->bhlv', q_, c_mat)
        den = jnp.sum(s, axis=-1) + w_inter * jnp.einsum('bhld,bhd->bhl', q_, n_vec)
        h = num / jnp.maximum(jnp.abs(den), jnp.exp(-m_q))[..., None]
        b_last = b[..., -1]
        gain = b_last[..., None] - b + ig
        m_new = jnp.maximum(b_last + m, jnp.max(gain, axis=-1))
        wk = jnp.exp(gain - m_new[..., None])
        decay = jnp.exp(b_last + m - m_new)
        c_mat = decay[..., None, None] * c_mat + jnp.einsum('bhs,bhsd,bhsv->bhdv', wk, k_, v_)
        n_vec = decay[..., None] * n_vec + jnp.einsum('bhs,bhsd->bhd', wk, k_)
        return (c_mat, n_vec, m_new), h

    init = (jnp.zeros((bsz, MLSTM_HEADS, DK, DV), jnp.float32),
            jnp.zeros((bsz, MLSTM_HEADS, DK), jnp.float32),
            jnp.zeros((bsz, MLSTM_HEADS), jnp.float32))
    _, hc = lax.scan(step, init, (qc, kc, vc, ic, lfc))
    return hc.transpose(1, 0, 3, 2, 4).reshape(bsz, seq, MLSTM_HEADS, DV)


def clamped_swiglu(a, u):
    a = jnp.minimum(a, SWIGLU_LIMIT)
    u = jnp.clip(u, -SWIGLU_LIMIT, SWIGLU_LIMIT)
    return a * jax.nn.sigmoid(SWIGLU_ALPHA * a) * (u + 1.0)


def moe_ffn(h, router_w, router_b, w_gate, b_gate, w_up, b_up, w_down, b_down):
    bsz, seq, d = h.shape
    n_tok = bsz * seq
    n_assign = n_tok * TOP_K
    n_blocks = -(-n_assign // MOE_BLOCK) + N_EXPERTS
    xt = h.reshape(n_tok, d)
    logits = (xt @ router_w + router_b).astype(jnp.float32)
    top_logit, top_idx = lax.top_k(logits, TOP_K)
    top_w = jax.nn.softmax(top_logit, axis=-1)
    e_flat = top_idx.reshape(n_assign)
    t_flat = jnp.arange(n_assign, dtype=jnp.int32) // TOP_K
    w_flat = top_w.reshape(n_assign)
    order = jnp.argsort(e_flat)
    e_sorted = e_flat[order]
    counts = jnp.bincount(e_flat, length=N_EXPERTS)
    starts = jnp.cumsum(counts) - counts
    padded = (counts + MOE_BLOCK - 1) // MOE_BLOCK * MOE_BLOCK
    pad_end = jnp.cumsum(padded)
    pad_start = pad_end - padded
    dest = pad_start[e_sorted] + jnp.arange(n_assign, dtype=jnp.int32) - starts[e_sorted]
    row_tok = jnp.full((n_blocks * MOE_BLOCK,), n_tok, jnp.int32).at[dest].set(t_flat[order])
    row_w = jnp.zeros((n_blocks * MOE_BLOCK,), jnp.float32).at[dest].set(w_flat[order])
    block_expert = jnp.minimum(
        jnp.searchsorted(pad_end, jnp.arange(n_blocks, dtype=jnp.int32) * MOE_BLOCK, side='right'),
        N_EXPERTS - 1)
    x_pad = jnp.concatenate([xt, jnp.zeros((1, d), xt.dtype)], axis=0)

    def expert_block(acc, blk):
        tok, wt, e = blk
        xb = x_pad[tok]
        act = clamped_swiglu(xb @ w_gate[e] + b_gate[e], xb @ w_up[e] + b_up[e])
        yb = act @ w_down[e] + b_down[e]
        return acc.at[tok].add(wt[:, None].astype(yb.dtype) * yb), None

    acc, _ = lax.scan(expert_block, jnp.zeros((n_tok + 1, d), xt.dtype),
                      (row_tok.reshape(n_blocks, MOE_BLOCK), row_w.reshape(n_blocks, MOE_BLOCK), block_expert))
    return acc[:n_tok].reshape(bsz, seq, d)


def hybrid_layer(x, c, w_ada, b_ada, w_in, b_in, dw_w, dw_b, conv_ln_g, conv_ln_b, mh_g, w_out,
                 ln1_g, ln1_b, router_w, router_b, w_gate, b_gate, w_up, b_up, w_down, b_down, ln2_g, ln2_b):
    bsz, seq, _ = x.shape
    mod = jax.nn.silu(c) @ w_ada + b_ada
    sh1, sc1, g1, sh2, sc2, g2 = jnp.split(mod[:, None, :], 6, axis=-1)
    h = layer_norm(x) * (1.0 + sc1) + sh1
    z = h @ w_in + b_in
    a_c, g_c, q, k, v, o, ig, fg = jnp.split(z, SPLIT_POINTS, axis=-1)
    y_conv = conformer_conv(a_c, g_c, dw_w, dw_b, conv_ln_g, conv_ln_b)
    hm = mlstm_chunkwise(q.reshape(bsz, seq, MLSTM_HEADS, DK), k.reshape(bsz, seq, MLSTM_HEADS, DK),
                         v.reshape(bsz, seq, MLSTM_HEADS, DV), ig, fg)
    hm = hm * lax.rsqrt(jnp.mean(jnp.square(hm), axis=-1, keepdims=True) + LN_EPS)
    hm = hm * mh_g.reshape(MLSTM_HEADS, DV).astype(jnp.float32)
    y_mlstm = (hm.reshape(bsz, seq, D_MLSTM) * jax.nn.sigmoid(o.astype(jnp.float32))).astype(x.dtype)
    y_mix = jnp.concatenate([y_conv, y_mlstm], axis=-1) @ w_out
    x = layer_norm(DEEPNORM_ALPHA * x + g1 * y_mix, ln1_g, ln1_b)
    h2 = layer_norm(x) * (1.0 + sc2) + sh2
    y_moe = moe_ffn(h2, router_w, router_b, w_gate, b_gate, w_up, b_up, w_down, b_down)
    return layer_norm(DEEPNORM_ALPHA * x + g2 * y_moe, ln2_g, ln2_b)


def setup_inputs(seed: int = 0) -> dict:
    key = jax.random.key(seed)
    ks = jax.random.split(key, 24)

    def nrm(k, shape, scale):
        return jax.random.normal(k, shape, jnp.float32) * scale

    b_in = nrm(ks[5], (DEPTH, D_IN), 0.01)
    b_in = b_in.at[:, D_IN - MLSTM_HEADS:].add(jnp.linspace(3.0, 6.0, MLSTM_HEADS))
    return {
        'x': nrm(ks[0], (BATCH, SEQ, D_MODEL), 1.0),
        'c': nrm(ks[1], (BATCH, D_MODEL), 1.0),
        'w_ada': nrm(ks[2], (DEPTH, D_MODEL, 6 * D_MODEL), D_MODEL ** -0.5),
        'b_ada': nrm(ks[3], (DEPTH, 6 * D_MODEL), 0.01),
        'w_in': nrm(ks[4], (DEPTH, D_MODEL, D_IN), D_MODEL ** -0.5),
        'b_in': b_in,
        'dw_w': nrm(ks[6], (DEPTH, CONV_WIDTH, D_CONV), CONV_WIDTH ** -0.5),
        'dw_b': nrm(ks[7], (DEPTH, D_CONV), 0.01),
        'conv_ln_g': 1.0 + nrm(ks[8], (DEPTH, D_CONV), 0.01),
        'conv_ln_b': nrm(ks[9], (DEPTH, D_CONV), 0.01),
        'mh_g': 1.0 + nrm(ks[10], (DEPTH, D_MLSTM), 0.01),
        'w_out': nrm(ks[11], (DEPTH, D_MIX, D_MODEL), D_MIX ** -0.5 * DEEPNORM_BETA),
        'ln1_g': 1.0 + nrm(ks[12], (DEPTH, D_MODEL), 0.01),
        'ln1_b': nrm(ks[13], (DEPTH, D_MODEL), 0.01),
        'router_w': nrm(ks[14], (DEPTH, D_MODEL, N_EXPERTS), D_MODEL ** -0.5),
        'router_b': nrm(ks[15], (DEPTH, N_EXPERTS), 0.01),
        'w_gate': nrm(ks[16], (DEPTH, N_EXPERTS, D_MODEL, D_EXPERT), D_MODEL ** -0.5),
        'b_gate': nrm(ks[17], (DEPTH, N_EXPERTS, D_EXPERT), 0.01),
        'w_up': nrm(ks[18], (DEPTH, N_EXPERTS, D_MODEL, D_EXPERT), D_MODEL ** -0.5),
        'b_up': nrm(ks[19], (DEPTH, N_EXPERTS, D_EXPERT), 0.01),
        'w_down': nrm(ks[20], (DEPTH, N_EXPERTS, D_EXPERT, D_MODEL), D_EXPERT ** -0.5 * DEEPNORM_BETA),
        'b_down': nrm(ks[21], (DEPTH, N_EXPERTS, D_MODEL), 0.01 * DEEPNORM_BETA),
        'ln2_g': 1.0 + nrm(ks[22], (DEPTH, D_MODEL), 0.01),
        'ln2_b': nrm(ks[23], (DEPTH, D_MODEL), 0.01),
    }


def reference(x, c, w_ada, b_ada, w_in, b_in, dw_w, dw_b, conv_ln_g, conv_ln_b, mh_g, w_out,
              ln1_g, ln1_b, router_w, router_b, w_gate, b_gate, w_up, b_up, w_down, b_down, ln2_g, ln2_b):
    for l in range(DEPTH):
        x = hybrid_layer(x, c, w_ada[l], b_ada[l], w_in[l], b_in[l], dw_w[l], dw_b[l], conv_ln_g[l], conv_ln_b[l],
                         mh_g[l], w_out[l], ln1_g[l], ln1_b[l], router_w[l], router_b[l], w_gate[l], b_gate[l],
                         w_up[l], b_up[l], w_down[l], b_down[l], ln2_g[l], ln2_b[l])
    return x
```

```python
import functools

import jax
import jax.numpy as jnp
from jax import lax
from jax.experimental import pallas as pl
from jax.experimental.pallas import tpu as pltpu

F32 = jnp.float32
BF16 = jnp.bfloat16
U32 = jnp.uint32
I32 = jnp.int32

CONV_WIDTH = 31
MLSTM_HEADS = 4
N_EXPERTS = 32
TOP_K = 4
SWIGLU_ALPHA = 1.702
SWIGLU_LIMIT = 7.0
LN_EPS = 1e-5
LANES = 128
SUBLANES = 8
CONV_HALO = 32
NEG_BIG = -1e30
HI16 = 0xFFFF0000
VMEM_LIMIT = 56 * 1024 * 1024


def _sigmoid(x):
    return 1.0 / (1.0 + jnp.exp(-x))


def _ln_rows(x):
    mu = jnp.mean(x, axis=-1, keepdims=True)
    xc = x - mu
    var = jnp.mean(xc * xc, axis=-1, keepdims=True)
    return xc * lax.rsqrt(var + LN_EPS)


def _pack_bf16_pairs(x):
    n = x.shape[1] // 2
    bits = pltpu.bitcast(x.astype(BF16).astype(F32), U32)
    return (bits[:, :n] >> 16) | bits[:, n:]


def _unpack_lo(u):
    return pltpu.bitcast(u << 16, F32)


def _unpack_hi(u):
    return pltpu.bitcast(u & jnp.uint32(HI16), F32)


def _params(*sem):
    return pltpu.CompilerParams(dimension_semantics=sem, vmem_limit_bytes=VMEM_LIMIT)


def _ada_body(c_ref, w_ref, b_ref, o_ref):
    c = c_ref[...]
    a = (c * _sigmoid(c)).astype(BF16)
    o_ref[...] = jnp.dot(a, w_ref[...].astype(BF16), preferred_element_type=F32) + b_ref[...]


def _ada(c, w, b, tn):
    bsz, d = c.shape
    n = w.shape[1]
    return pl.pallas_call(
        _ada_body,
        grid=(n // tn,),
        in_specs=[pl.BlockSpec((bsz, d), lambda j: (0, 0)),
                  pl.BlockSpec((d, tn), lambda j: (0, j)),
                  pl.BlockSpec((1, tn), lambda j: (0, j))],
        out_specs=pl.BlockSpec((bsz, tn), lambda j: (0, j)),
        out_shape=jax.ShapeDtypeStruct((bsz, n), F32),
        compiler_params=_params("arbitrary"),
        name="ada",
    )(c, w, b.reshape(1, n))


def _inproj_body(x_ref, sc_ref, sh_ref, w_ref, b_ref, wg_ref, bg_ref, z_ref, g_ref, h_scr, *, row_chunk):
    j = pl.program_id(1)
    tm = x_ref.shape[0]

    @pl.when(j == 0)
    def _():
        scale = 1.0 + sc_ref[0]
        shift = sh_ref[0]

        def chunk(r, carry):
            rows = pl.ds(pl.multiple_of(r * row_chunk, row_chunk), row_chunk)
            h = _ln_rows(x_ref[rows, :]) * scale + shift
            h_scr[rows, :] = h.astype(BF16)
            return carry

        lax.fori_loop(0, tm // row_chunk, chunk, 0)
        g_ref[...] = jnp.dot(h_scr[...], wg_ref[...], preferred_element_type=F32) + bg_ref[...]

    z = jnp.dot(h_scr[...], w_ref[...], preferred_element_type=F32) + b_ref[...]
    z_ref[...] = z.astype(z_ref.dtype)


def _inproj(x2d, sc, sh, w, b, wg, bg, *, seq, tm, tn):
    t, d = x2d.shape
    n = w.shape[1]
    per_batch = seq // tm
    body = functools.partial(_inproj_body, row_chunk=min(tm, 128))
    return pl.pallas_call(
        body,
        grid=(t // tm, n // tn),
        in_specs=[pl.BlockSpec((tm, d), lambda i, j: (i, 0)),
                  pl.BlockSpec((1, 1, d), lambda i, j: (i // per_batch, 0, 0)),
                  pl.BlockSpec((1, 1, d), lambda i, j: (i // per_batch, 0, 0)),
                  pl.BlockSpec((d, tn), lambda i, j: (0, j)),
                  pl.BlockSpec((1, tn), lambda i, j: (0, j)),
                  pl.BlockSpec((d, LANES), lambda i, j: (0, 0)),
                  pl.BlockSpec((1, LANES), lambda i, j: (0, 0))],
        out_specs=[pl.BlockSpec((tm, tn), lambda i, j: (i, j)),
                   pl.BlockSpec((tm, LANES), lambda i, j: (i, 0))],
        out_shape=[jax.ShapeDtypeStruct((t, n), BF16), jax.ShapeDtypeStruct((t, LANES), F32)],
        scratch_shapes=[pltpu.VMEM((tm, d), BF16)],
        compiler_params=_params("arbitrary", "arbitrary"),
        name="inproj",
    )(x2d, sc, sh, w, b, wg, bg)


def _conv_body(vc_ref, gc_ref, vh_ref, gh_ref, w_ref, b_ref, lg_ref, lb_ref, o_ref, u_scr, y_scr, *, rc):
    i = pl.program_id(1)
    ts = vc_ref.shape[1]
    ch = vc_ref.shape[2]
    uh = vh_ref[0].astype(F32) * _sigmoid(gh_ref[0].astype(F32))
    u_scr[0:CONV_HALO, :] = jnp.where(i > 0, uh, 0.0)

    def glu_chunk(r, carry):
        r0 = pl.multiple_of(r * rc, rc)
        v = vc_ref[0, pl.ds(r0, rc), :].astype(F32)
        g = gc_ref[0, pl.ds(r0, rc), :].astype(F32)
        u_scr[pl.ds(CONV_HALO + r0, rc), :] = v * _sigmoid(g)
        return carry

    lax.fori_loop(0, ts // rc, glu_chunk, 0)

    first_off = CONV_HALO - (CONV_WIDTH - 1)
    n_lane_chunks = ch // LANES

    def conv_chunk(idx, carry):
        r0 = pl.multiple_of((idx // n_lane_chunks) * rc, rc)
        l0 = pl.multiple_of((idx % n_lane_chunks) * LANES, LANES)
        lanes = pl.ds(l0, LANES)
        y = None
        for p in range(SUBLANES):
            win = rc if p == 0 else rc + SUBLANES
            v = None
            for a in range(CONV_HALO // SUBLANES + 1):
                off = SUBLANES * a + p
                if off < first_off or off > CONV_HALO:
                    continue
                tap = off - first_off
                term = w_ref[tap:tap + 1, lanes] * u_scr[pl.ds(r0 + SUBLANES * a, win), lanes]
                v = term if v is None else v + term
            vs = v if p == 0 else v[p:p + rc, :]
            y = vs if y is None else y + vs
        y_scr[pl.ds(r0, rc), lanes] = y + b_ref[:, lanes]
        return carry

    lax.fori_loop(0, (ts // rc) * n_lane_chunks, conv_chunk, 0)

    def ln_chunk(r, carry):
        r0 = pl.multiple_of(r * rc, rc)
        yn = _ln_rows(y_scr[pl.ds(r0, rc), :]) * lg_ref[...] + lb_ref[...]
        o_ref[0, pl.ds(r0, rc), :] = (yn * _sigmoid(yn)).astype(o_ref.dtype)
        return carry

    lax.fori_loop(0, ts // rc, ln_chunk, 0)


def _conv(z3, dw_w, dw_b, ln_g, ln_b, *, ts, ch):
    bsz, seq, _ = z3.shape
    halo_blocks = ts // CONV_HALO
    body = functools.partial(_conv_body, rc=min(ts, 64))
    row = lambda a: a.reshape(1, ch)
    return pl.pallas_call(
        body,
        grid=(bsz, seq // ts),
        in_specs=[pl.BlockSpec((1, ts, ch), lambda b, i: (b, i, 0)),
                  pl.BlockSpec((1, ts, ch), lambda b, i: (b, i, 1)),
                  pl.BlockSpec((1, CONV_HALO, ch), lambda b, i: (b, jnp.maximum(i * halo_blocks - 1, 0), 0)),
                  pl.BlockSpec((1, CONV_HALO, ch),
                               lambda b, i: (b, jnp.maximum(i * halo_blocks - 1, 0), ch // ch)),
                  pl.BlockSpec((CONV_WIDTH, ch), lambda b, i: (0, 0)),
                  pl.BlockSpec((1, ch), lambda b, i: (0, 0)),
                  pl.BlockSpec((1, ch), lambda b, i: (0, 0)),
                  pl.BlockSpec((1, ch), lambda b, i: (0, 0))],
        out_specs=pl.BlockSpec((1, ts, ch), lambda b, i: (b, i, 0)),
        out_shape=jax.ShapeDtypeStruct((bsz, seq, ch), BF16),
        scratch_shapes=[pltpu.VMEM((ts + CONV_HALO, ch), F32), pltpu.VMEM((ts, ch), F32)],
        compiler_params=_params("arbitrary", "arbitrary"),
        name="conv",
    )(z3, z3, z3, z3, dw_w, row(dw_b), row(ln_g), row(ln_b))


def _split3_dot(tri, x):
    hi = x.astype(BF16)
    r1 = x - hi.astype(F32)
    mid = r1.astype(BF16)
    lo = (r1 - mid.astype(F32)).astype(BF16)
    dot = lambda p: jnp.dot(tri, p, preferred_element_type=F32)
    return dot(hi) + dot(mid) + dot(lo)


def _mlstm_body(q_ref, k_ref, v_ref, o_ref, g_ref, mhg_ref, y_ref, c_scr, n_scr, m_scr, *, dk, dv):
    heads = MLSTM_HEADS
    ln = q_ref.shape[1]
    scale = dk ** -0.5

    @pl.when(pl.program_id(1) == 0)
    def _():
        c_scr[...] = jnp.zeros_like(c_scr)
        n_scr[...] = jnp.zeros_like(n_scr)
        m_scr[...] = jnp.zeros_like(m_scr)

    g = g_ref[0]
    lane = lax.broadcasted_iota(I32, g.shape, 1)
    lf = jnp.minimum(g, 0.0) - jnp.log1p(jnp.exp(-jnp.abs(g)))
    row_i = lax.broadcasted_iota(I32, (ln, ln), 0)
    col_i = lax.broadcasted_iota(I32, (ln, ln), 1)
    causal = col_i <= row_i
    tri = jnp.where(causal, 1.0, 0.0).astype(BF16)
    b = _split3_dot(tri, lf)
    cmb_t = jnp.where(lane < heads, g, b).T

    for h in range(heads):
        ig_col = g[:, h:h + 1]
        b_col = b[:, heads + h:heads + h + 1]
        ig_row = cmb_t[h:h + 1, :]
        b_row = cmb_t[heads + h:heads + h + 1, :]
        m_prev = m_scr[h][0:1, 0:1]
        dmat = jnp.where(causal, b_col - b_row + ig_row, -jnp.inf)
        inter = b_col + m_prev
        m_q = jnp.maximum(inter, jnp.max(dmat, axis=-1, keepdims=True))
        p = jnp.exp(dmat - m_q)
        qh = q_ref[0, :, h * dk:(h + 1) * dk]
        kh = k_ref[0, :, h * dk:(h + 1) * dk]
        vh = v_ref[0, :, h * dv:(h + 1) * dv]
        qk = lax.dot_general(qh, kh, (((1,), (1,)), ((), ())), preferred_element_type=F32)
        s = qk * (p * scale)
        w_inter = jnp.exp(inter - m_q)
        c_h = c_scr[h]
        n_h = n_scr[h]
        qc = jnp.dot(qh, c_h.astype(BF16), preferred_element_type=F32)
        num = jnp.dot(s.astype(BF16), vh, preferred_element_type=F32) + w_inter * qc
        qn = jnp.sum(qh.astype(F32) * n_h, axis=-1, keepdims=True)
        den = jnp.sum(s, axis=-1, keepdims=True) + w_inter * qn
        hh = num / jnp.maximum(jnp.abs(den), jnp.exp(-m_q))
        ms = jnp.mean(hh * hh, axis=-1, keepdims=True)
        hn = hh * lax.rsqrt(ms + LN_EPS) * mhg_ref[:, h * dv:(h + 1) * dv]
        og = o_ref[0, :, h * dv:(h + 1) * dv].astype(F32)
        y_ref[0, :, h * dv:(h + 1) * dv] = (hn * _sigmoid(og)).astype(y_ref.dtype)

        b_last = b[ln - 1:ln, heads + h:heads + h + 1]
        gain = b_last - b_col + ig_col
        m_new = jnp.maximum(b_last + m_prev, jnp.max(gain, axis=0, keepdims=True))
        wk = jnp.exp(gain - m_new) * scale
        decay = jnp.exp(b_last + m_prev - m_new)
        kw = kh.astype(F32) * wk
        kv = lax.dot_general(kw.astype(BF16), vh, (((0,), (0,)), ((), ())), preferred_element_type=F32)
        c_scr[h] = decay * c_h + kv
        n_scr[h] = decay * n_h + jnp.sum(kw, axis=0, keepdims=True)
        m_scr[h] = jnp.broadcast_to(m_new, (SUBLANES, LANES))


def _mlstm(z3, gates3, mh_g, *, ln, dk, dv, qk_col, v_col, o_col):
    bsz, seq, _ = z3.shape
    heads = MLSTM_HEADS
    body = functools.partial(_mlstm_body, dk=dk, dv=dv)
    return pl.pallas_call(
        body,
        grid=(bsz, seq // ln),
        in_specs=[pl.BlockSpec((1, ln, heads * dk), lambda b, i: (b, i, 2 * qk_col)),
                  pl.BlockSpec((1, ln, heads * dk), lambda b, i: (b, i, 2 * qk_col + 1)),
                  pl.BlockSpec((1, ln, heads * dv), lambda b, i: (b, i, v_col)),
                  pl.BlockSpec((1, ln, heads * dv), lambda b, i: (b, i, o_col)),
                  pl.BlockSpec((1, ln, LANES), lambda b, i: (b, i, 0)),
                  pl.BlockSpec((1, heads * dv), lambda b, i: (0, 0))],
        out_specs=pl.BlockSpec((1, ln, heads * dv), lambda b, i: (b, i, 0)),
        out_shape=jax.ShapeDtypeStruct((bsz, seq, heads * dv), BF16),
        scratch_shapes=[pltpu.VMEM((heads, dk, dv), F32),
                        pltpu.VMEM((heads, 1, dk), F32),
                        pltpu.VMEM((heads, SUBLANES, LANES), F32)],
        compiler_params=_params("arbitrary", "arbitrary"),
        name="mlstm",
    )(z3, z3, z3, z3, gates3, mh_g.reshape(1, heads * dv))


def _lane_pick(cols):
    rows = cols[0].shape[0]
    lane = lax.broadcasted_iota(I32, (rows, LANES), 1)
    out = jnp.zeros((rows, LANES), cols[0].dtype)
    for k, col in enumerate(cols):
        out = jnp.where(lane == k, col, out)
    return out


def _outproj_body(yc_ref, ym_ref, x_ref, g1_ref, l1g_ref, l1b_ref, sc2_ref, sh2_ref, wo_ref, rwh_ref, rwl_ref,
                  rb_ref, x1_ref, h2p_ref, idx_ref, tw_ref, pos_ref, cnt_ref, carry_scr, *, alpha):
    @pl.when(pl.program_id(0) == 0)
    def _():
        carry_scr[...] = jnp.zeros_like(carry_scr)

    tm = x_ref.shape[0]
    dc = yc_ref.shape[1]
    ymix = (jnp.dot(yc_ref[...], wo_ref[0:dc, :], preferred_element_type=F32)
            + jnp.dot(ym_ref[...], wo_ref[dc:, :], preferred_element_type=F32))
    x1 = _ln_rows(alpha * x_ref[...] + g1_ref[0] * ymix) * l1g_ref[...] + l1b_ref[...]
    x1_ref[...] = x1
    h2 = _ln_rows(x1) * (1.0 + sc2_ref[0]) + sh2_ref[0]
    h2p_ref[...] = _pack_bf16_pairs(h2)

    h2_hi = h2.astype(BF16)
    h2_lo = (h2 - h2_hi.astype(F32)).astype(BF16)
    rdot = lambda a, w: jnp.dot(a, w[...], preferred_element_type=F32)
    logits = rdot(h2_hi, rwh_ref) + rdot(h2_hi, rwl_ref) + rdot(h2_lo, rwh_ref) + rb_ref[...]

    lane_f = lax.broadcasted_iota(I32, (tm, LANES), 1).astype(F32)
    work = logits
    sel_idx, sel_val = [], []
    for _ in range(TOP_K):
        mx = jnp.max(work, axis=-1, keepdims=True)
        am = jnp.min(jnp.where(work == mx, lane_f, float(LANES)), axis=-1, keepdims=True)
        sel_idx.append(am)
        sel_val.append(mx)
        work = jnp.where(lane_f == am, NEG_BIG, work)
    ex = [jnp.exp(v - sel_val[0]) for v in sel_val]
    denom = ex[0] + ex[1] + ex[2] + ex[3]
    tw_ref[...] = _lane_pick([e / denom for e in ex])
    idx_ref[...] = _lane_pick([a.astype(I32) for a in sel_idx])

    onehot = jnp.zeros((tm, LANES), F32)
    for a in sel_idx:
        onehot = onehot + jnp.where(lane_f == a, 1.0, 0.0)
    row_i = lax.broadcasted_iota(I32, (tm, tm), 0)
    col_i = lax.broadcasted_iota(I32, (tm, tm), 1)
    strict = jnp.where(col_i < row_i, 1.0, 0.0).astype(BF16)
    before = jnp.dot(strict, onehot.astype(BF16), preferred_element_type=F32) + carry_scr[0:1, :]
    pos = [jnp.sum(jnp.where(lane_f == a, before, 0.0), axis=-1, keepdims=True).astype(I32) for a in sel_idx]
    pos_ref[...] = _lane_pick(pos)
    total = carry_scr[0:1, :] + jnp.sum(onehot, axis=0, keepdims=True)
    carry_scr[...] = jnp.broadcast_to(total, carry_scr.shape)
    cnt_ref[...] = jnp.broadcast_to(total, cnt_ref.shape)


def _outproj(yc, ym, x2d, g1, l1g, l1b, sc2, sh2, wo, rwh, rwl, rb, *, seq, tm, alpha):
    t, d = x2d.shape
    dc = yc.shape[1]
    per_batch = seq // tm
    body = functools.partial(_outproj_body, alpha=alpha)
    full = lambda shape: pl.BlockSpec(shape, lambda i: tuple(0 for _ in shape))
    per_b = pl.BlockSpec((1, 1, d), lambda i: (i // per_batch, 0, 0))
    rows = lambda w: pl.BlockSpec((tm, w), lambda i: (i, 0))
    return pl.pallas_call(
        body,
        grid=(t // tm,),
        in_specs=[rows(dc), rows(dc), rows(d), per_b, full((1, d)), full((1, d)), per_b, per_b,
                  full((d, d)), full((d, LANES)), full((d, LANES)), full((1, LANES))],
        out_specs=[rows(d), rows(d // 2), rows(LANES), rows(LANES), rows(LANES), full((SUBLANES, LANES))],
        out_shape=[jax.ShapeDtypeStruct((t, d), F32), jax.ShapeDtypeStruct((t, d // 2), U32),
                   jax.ShapeDtypeStruct((t, LANES), I32), jax.ShapeDtypeStruct((t, LANES), F32),
                   jax.ShapeDtypeStruct((t, LANES), I32), jax.ShapeDtypeStruct((SUBLANES, LANES), F32)],
        scratch_shapes=[pltpu.VMEM((SUBLANES, LANES), F32)],
        compiler_params=_params("arbitrary"),
        name="outproj",
    )(yc, ym, x2d, g1, l1g, l1b, sc2, sh2, wo, rwh, rwl, rb)


def _dispatch_body(idx_ref, pos_ref, start_ref, h2p_ref, xs_in_ref, xs_ref, sem):
    del xs_in_ref
    tr = h2p_ref.shape[0]

    def copy(r, d):
        return pltpu.make_async_copy(h2p_ref.at[pl.ds(r, 1)], xs_ref.at[pl.ds(d, 1)], sem)

    def issue(r, carry):
        for k in range(TOP_K):
            a = r * TOP_K + k
            copy(r, start_ref[idx_ref[a]] + pos_ref[a]).start()
        return carry

    lax.fori_loop(0, tr, issue, 0)

    def drain(r, carry):
        for _ in range(TOP_K):
            copy(0, 0).wait()
        return carry

    lax.fori_loop(0, tr, drain, 0)


def _dispatch(idx_flat, pos_flat, seg_start, h2p, xs_zero, *, tr):
    t, half = h2p.shape
    smem_rows = pl.BlockSpec((tr * TOP_K,), lambda i: (i,), memory_space=pltpu.SMEM)
    return pl.pallas_call(
        _dispatch_body,
        grid=(t // tr,),
        in_specs=[smem_rows, smem_rows,
                  pl.BlockSpec(memory_space=pltpu.SMEM),
                  pl.BlockSpec((tr, half), lambda i: (i, 0)),
                  pl.BlockSpec(memory_space=pl.ANY)],
        out_specs=pl.BlockSpec(memory_space=pl.ANY),
        out_shape=jax.ShapeDtypeStruct(xs_zero.shape, xs_zero.dtype),
        scratch_shapes=[pltpu.SemaphoreType.DMA(())],
        input_output_aliases={4: 0},
        compiler_params=_params("arbitrary"),
        name="dispatch",
    )(idx_flat, pos_flat, seg_start, h2p, xs_zero)


def _moe_body(be_ref, bs_ref, nu_ref, xs_ref, wg_ref, bg_ref, wu_ref, bu_ref, wd_ref, bd_ref, ys_ref,
              xb_scr, acc_scr):
    del be_ref, bs_ref
    i = pl.program_id(0)
    j = pl.program_id(1)
    half = xs_ref.shape[1]

    @pl.when(i < nu_ref[0])
    def _():
        @pl.when(j == 0)
        def _():
            u = xs_ref[...]
            xb_scr[:, :half] = _unpack_lo(u).astype(BF16)
            xb_scr[:, half:] = _unpack_hi(u).astype(BF16)
            acc_scr[...] = jnp.zeros_like(acc_scr)

        xb = xb_scr[...]
        a = jnp.dot(xb, wg_ref[0].astype(BF16), preferred_element_type=F32) + bg_ref[0]
        u = jnp.dot(xb, wu_ref[0].astype(BF16), preferred_element_type=F32) + bu_ref[0]
        a = jnp.minimum(a, SWIGLU_LIMIT)
        u = jnp.clip(u, -SWIGLU_LIMIT, SWIGLU_LIMIT)
        act = a * _sigmoid(SWIGLU_ALPHA * a) * (u + 1.0)
        acc_scr[...] += jnp.dot(act.astype(BF16), wd_ref[0].astype(BF16), preferred_element_type=F32)

        @pl.when(j == pl.num_programs(1) - 1)
        def _():
            ys_ref[...] = _pack_bf16_pairs(acc_scr[...] + bd_ref[0])


def _moe(blk_e, blk_src, n_used, xs, w_gate, b_gate, w_up, b_up, w_down, b_down, *, rb, tn):
    nr, half = xs.shape
    e, d, de = w_gate.shape
    nj = de // tn

    def jj(i, j, nu):
        return jnp.where(i < nu[0], j, nj - 1)

    grid_spec = pltpu.PrefetchScalarGridSpec(
        num_scalar_prefetch=3,
        grid=(nr // rb, nj),
        in_specs=[pl.BlockSpec((rb, half), lambda i, j, be, bs, nu: (bs[i], 0)),
                  pl.BlockSpec((1, d, tn), lambda i, j, be, bs, nu: (be[i], 0, jj(i, j, nu))),
                  pl.BlockSpec((1, 1, tn), lambda i, j, be, bs, nu: (be[i], 0, jj(i, j, nu))),
                  pl.BlockSpec((1, d, tn), lambda i, j, be, bs, nu: (be[i], 0, jj(i, j, nu))),
                  pl.BlockSpec((1, 1, tn), lambda i, j, be, bs, nu: (be[i], 0, jj(i, j, nu))),
                  pl.BlockSpec((1, tn, d), lambda i, j, be, bs, nu: (be[i], jj(i, j, nu), 0)),
                  pl.BlockSpec((1, 1, d), lambda i, j, be, bs, nu: (be[i], 0, 0))],
        out_specs=pl.BlockSpec((rb, half), lambda i, j, be, bs, nu: (bs[i], 0)),
        scratch_shapes=[pltpu.VMEM((rb, d), BF16), pltpu.VMEM((rb, d), F32)],
    )
    return pl.pallas_call(
        _moe_body,
        grid_spec=grid_spec,
        out_shape=jax.ShapeDtypeStruct((nr, half), U32),
        compiler_params=_params("arbitrary", "arbitrary"),
        name="moe",
    )(blk_e, blk_src, n_used, xs, w_gate, b_gate.reshape(e, 1, de), w_up, b_up.reshape(e, 1, de),
      w_down, b_down.reshape(e, 1, d))


def _combine_body(idx_ref, pos_ref, start_ref, tw_ref, x1_ref, g2_ref, lg_ref, lb_ref, ys_ref, o_ref,
                  ybuf, sem, *, alpha):
    tc = x1_ref.shape[0]
    d = x1_ref.shape[1]
    half = d // 2

    def copy(r, k, src_row):
        return pltpu.make_async_copy(ys_ref.at[pl.ds(src_row, 1)], ybuf.at[k, pl.ds(r, 1)], sem)

    def issue(r, carry):
        for k in range(TOP_K):
            a = r * TOP_K + k
            copy(r, k, start_ref[idx_ref[a]] + pos_ref[a]).start()
        return carry

    lax.fori_loop(0, tc, issue, 0)

    def drain(r, carry):
        for k in range(TOP_K):
            copy(0, k, 0).wait()
        return carry

    lax.fori_loop(0, tc, drain, 0)

    tw = tw_ref[...]
    acc_lo = jnp.zeros((tc, half), F32)
    acc_hi = jnp.zeros((tc, half), F32)
    for k in range(TOP_K):
        u = ybuf[k]
        wk = tw[:, k:k + 1]
        acc_lo = acc_lo + wk * _unpack_lo(u)
        acc_hi = acc_hi + wk * _unpack_hi(u)
    g2 = g2_ref[0]
    r_lo = alpha * x1_ref[:, :half] + g2[:, :half] * acc_lo
    r_hi = alpha * x1_ref[:, half:] + g2[:, half:] * acc_hi
    mu = (jnp.sum(r_lo, axis=-1, keepdims=True) + jnp.sum(r_hi, axis=-1, keepdims=True)) / d
    c_lo = r_lo - mu
    c_hi = r_hi - mu
    var = (jnp.sum(c_lo * c_lo, axis=-1, keepdims=True) + jnp.sum(c_hi * c_hi, axis=-1, keepdims=True)) / d
    inv = lax.rsqrt(var + LN_EPS)
    o_ref[:, :half] = c_lo * inv * lg_ref[:, :half] + lb_ref[:, :half]
    o_ref[:, half:] = c_hi * inv * lg_ref[:, half:] + lb_ref[:, half:]


def _combine(idx_flat, pos_flat, seg_start, tw, x1, g2, lg, lb, ys, *, seq, tc, alpha):
    t, d = x1.shape
    per_batch = seq // tc
    body = functools.partial(_combine_body, alpha=alpha)
    smem_rows = pl.BlockSpec((tc * TOP_K,), lambda i: (i,), memory_space=pltpu.SMEM)
    return pl.pallas_call(
        body,
        grid=(t // tc,),
        in_specs=[smem_rows, smem_rows,
                  pl.BlockSpec(memory_space=pltpu.SMEM),
                  pl.BlockSpec((tc, LANES), lambda i: (i, 0)),
                  pl.BlockSpec((tc, d), lambda i: (i, 0)),
                  pl.BlockSpec((1, 1, d), lambda i: (i // per_batch, 0, 0)),
                  pl.BlockSpec((1, d), lambda i: (0, 0)),
                  pl.BlockSpec((1, d), lambda i: (0, 0)),
                  pl.BlockSpec(memory_space=pl.ANY)],
        out_specs=pl.BlockSpec((tc, d), lambda i: (i, 0)),
        out_shape=jax.ShapeDtypeStruct((t, d), F32),
        scratch_shapes=[pltpu.VMEM((TOP_K, tc, d // 2), U32), pltpu.SemaphoreType.DMA(())],
        compiler_params=_params("arbitrary"),
        name="combine",
    )(idx_flat, pos_flat, seg_start, tw, x1, g2, lg, lb, ys)


def _tiles(seq):
    return dict(
        ada_tn=1024,
        inproj_tm=min(seq, 1024), inproj_tn=512,
        conv_ts=min(seq, 512),
        mlstm_len=min(seq, 256),
        outproj_tm=min(seq, 512),
        dispatch_tr=min(seq, 512),
        moe_rows=1024, moe_tn=256,
        combine_tc=min(seq, 256),
    )


def _layer(x, c, w_ada, b_ada, w_in, b_in, dw_w, dw_b, conv_ln_g, conv_ln_b, mh_g, w_out, ln1_g, ln1_b,
           router_w, router_b, w_gate, b_gate, w_up, b_up, w_down, b_down, ln2_g, ln2_b, *, alpha, tiles):
    bsz, seq, d = x.shape
    t = bsz * seq
    heads = MLSTM_HEADS
    d_conv = dw_w.shape[1]
    d_mlstm = mh_g.shape[0]
    dv = d_mlstm // heads
    dk = dv // 2
    n_main = 2 * d_conv + 2 * heads * dk + 2 * d_mlstm
    assert w_in.shape[1] == n_main + 2 * heads and d_conv == d_mlstm == 2 * heads * dk

    mod = _ada(c, w_ada, b_ada, tiles["ada_tn"])
    sh1, sc1, g1, sh2, sc2, g2 = [m.reshape(bsz, 1, d) for m in jnp.split(mod, 6, axis=-1)]

    x2d = x.reshape(t, d)
    wg = jnp.pad(w_in[:, n_main:], ((0, 0), (0, LANES - 2 * heads))).astype(BF16)
    bg = jnp.pad(b_in[n_main:], (0, LANES - 2 * heads)).reshape(1, LANES)
    z, gates = _inproj(x2d, sc1, sh1, w_in[:, :n_main].astype(BF16), b_in[:n_main].reshape(1, n_main), wg, bg,
                       seq=seq, tm=tiles["inproj_tm"], tn=tiles["inproj_tn"])
    z3 = z.reshape(bsz, seq, n_main)

    y_conv = _conv(z3, dw_w, dw_b, conv_ln_g, conv_ln_b, ts=tiles["conv_ts"], ch=d_conv)
    y_mlstm = _mlstm(z3, gates.reshape(bsz, seq, LANES), mh_g, ln=tiles["mlstm_len"], dk=dk, dv=dv,
                     qk_col=2, v_col=3, o_col=4)

    rw = jnp.pad(router_w, ((0, 0), (0, LANES - N_EXPERTS)))
    rw_hi = rw.astype(BF16)
    rw_lo = (rw - rw_hi.astype(F32)).astype(BF16)
    rb = jnp.pad(router_b, (0, LANES - N_EXPERTS), constant_values=NEG_BIG).reshape(1, LANES)
    x1, h2p, top_idx, top_w, pos, cnt = _outproj(
        y_conv.reshape(t, d_conv), y_mlstm.reshape(t, d_mlstm), x2d, g1, ln1_g.reshape(1, d), ln1_b.reshape(1, d),
        sc2, sh2, w_out.astype(BF16), rw_hi, rw_lo, rb, seq=seq, tm=tiles["outproj_tm"], alpha=alpha)

    rb_rows = tiles["moe_rows"]
    n_blocks = -(-t * TOP_K // rb_rows) + N_EXPERTS
    counts = cnt[0, :N_EXPERTS].astype(I32)
    seg_blocks = (counts + rb_rows - 1) // rb_rows
    blocks_end = jnp.cumsum(seg_blocks)
    seg_start = ((blocks_end - seg_blocks) * rb_rows).astype(I32)
    n_used = blocks_end[-1:].astype(I32)
    blk = jnp.minimum(jnp.arange(n_blocks, dtype=I32), n_used[0] - 1)
    blk_e = jnp.minimum(jnp.searchsorted(blocks_end, blk, side="right"), N_EXPERTS - 1).astype(I32)

    idx_flat = top_idx[:, :TOP_K].reshape(t * TOP_K)
    pos_flat = pos[:, :TOP_K].reshape(t * TOP_K)
    xs = _dispatch(idx_flat, pos_flat, seg_start, h2p, jnp.zeros((n_blocks * rb_rows, d // 2), U32),
                   tr=tiles["dispatch_tr"])
    ys = _moe(blk_e, blk, n_used, xs, w_gate, b_gate, w_up, b_up, w_down, b_down, rb=rb_rows, tn=tiles["moe_tn"])
    out = _combine(idx_flat, pos_flat, seg_start, top_w, x1, g2, ln2_g.reshape(1, d), ln2_b.reshape(1, d), ys,
                   seq=seq, tc=tiles["combine_tc"], alpha=alpha)
    return out.reshape(bsz, seq, d)


def kernel(x, c, w_ada, b_ada, w_in, b_in, dw_w, dw_b, conv_ln_g, conv_ln_b, mh_g, w_out, ln1_g, ln1_b, router_w, router_b, w_gate, b_gate, w_up, b_up, w_down, b_down, ln2_g, ln2_b):
    depth = w_ada.shape[0]
    alpha = (2 * depth) ** 0.25
    tiles = _tiles(x.shape[1])
    for l in range(depth):
        x = _layer(x, c, w_ada[l], b_ada[l], w_in[l], b_in[l], dw_w[l], dw_b[l], conv_ln_g[l], conv_ln_b[l],
                   mh_g[l], w_out[l], ln1_g[l], ln1_b[l], router_w[l], router_b[l], w_gate[l], b_gate[l],
                   w_up[l], b_up[l], w_down[l], b_down[l], ln2_g[l], ln2_b[l], alpha=alpha, tiles=tiles)
    return x
```

```python
import functools

import jax
import jax.numpy as jnp
from jax import lax
from jax.experimental import pallas as pl
from jax.experimental.pallas import tpu as pltpu

F32 = jnp.float32
BF16 = jnp.bfloat16
I32 = jnp.int32

CONV_WIDTH = 31
MLSTM_HEADS = 4
N_EXPERTS = 32
TOP_K = 4
SWIGLU_ALPHA = 1.702
SWIGLU_LIMIT = 7.0
LN_EPS = 1e-5
LANES = 128
SUBLANES = 8
CONV_HALO = 32
MOE_SUB = 256
ISSUE_UNROLL = 8
NEG_BIG = -1e30
VMEM_LIMIT = 56 * 1024 * 1024


def _sigmoid(x):
    return 1.0 / (1.0 + jnp.exp(-x))


def _ln_rows(x):
    mu = jnp.mean(x, axis=-1, keepdims=True)
    xc = x - mu
    var = jnp.mean(xc * xc, axis=-1, keepdims=True)
    return xc * lax.rsqrt(var + LN_EPS)


def _row(ref, r):
    return ref.at[pl.ds(r, 1)]


def _params(*sem):
    return pltpu.CompilerParams(dimension_semantics=sem, vmem_limit_bytes=VMEM_LIMIT)


def _ada_body(c_ref, w_ref, b_ref, o_ref):
    c = c_ref[...]
    a = (c * _sigmoid(c)).astype(BF16)
    o_ref[...] = jnp.dot(a, w_ref[...].astype(BF16), preferred_element_type=F32) + b_ref[...]


def _ada(c, w, b, tn):
    bsz, d = c.shape
    n = w.shape[1]
    return pl.pallas_call(
        _ada_body,
        grid=(n // tn,),
        in_specs=[pl.BlockSpec((bsz, d), lambda j: (0, 0)),
                  pl.BlockSpec((d, tn), lambda j: (0, j)),
                  pl.BlockSpec((1, tn), lambda j: (0, j))],
        out_specs=pl.BlockSpec((bsz, tn), lambda j: (0, j)),
        out_shape=jax.ShapeDtypeStruct((bsz, n), F32),
        compiler_params=_params("arbitrary"),
        name="ada",
    )(c, w, b.reshape(1, n))


def _inproj_body(x_ref, sc_ref, sh_ref, w_ref, b_ref, wg_ref, bg_ref, z_ref, g_ref, h_scr, *, row_chunk):
    j = pl.program_id(1)
    tm = x_ref.shape[0]

    @pl.when(j == 0)
    def _():
        scale = 1.0 + sc_ref[0]
        shift = sh_ref[0]

        def chunk(r, carry):
            rows = pl.ds(pl.multiple_of(r * row_chunk, row_chunk), row_chunk)
            h = _ln_rows(x_ref[rows, :]) * scale + shift
            h_scr[rows, :] = h.astype(BF16)
            return carry

        lax.fori_loop(0, tm // row_chunk, chunk, 0)
        g_ref[...] = jnp.dot(h_scr[...], wg_ref[...], preferred_element_type=F32) + bg_ref[...]

    z = jnp.dot(h_scr[...], w_ref[...], preferred_element_type=F32) + b_ref[...]
    z_ref[...] = z.astype(z_ref.dtype)


def _inproj(x2d, sc, sh, w, b, wg, bg, *, seq, tm, tn):
    t, d = x2d.shape
    n = w.shape[1]
    per_batch = seq // tm
    body = functools.partial(_inproj_body, row_chunk=min(tm, 128))
    return pl.pallas_call(
        body,
        grid=(t // tm, n // tn),
        in_specs=[pl.BlockSpec((tm, d), lambda i, j: (i, 0)),
                  pl.BlockSpec((1, 1, d), lambda i, j: (i // per_batch, 0, 0)),
                  pl.BlockSpec((1, 1, d), lambda i, j: (i // per_batch, 0, 0)),
                  pl.BlockSpec((d, tn), lambda i, j: (0, j)),
                  pl.BlockSpec((1, tn), lambda i, j: (0, j)),
                  pl.BlockSpec((d, LANES), lambda i, j: (0, 0)),
                  pl.BlockSpec((1, LANES), lambda i, j: (0, 0))],
        out_specs=[pl.BlockSpec((tm, tn), lambda i, j: (i, j)),
                   pl.BlockSpec((tm, LANES), lambda i, j: (i, 0))],
        out_shape=[jax.ShapeDtypeStruct((t, n), BF16), jax.ShapeDtypeStruct((t, LANES), F32)],
        scratch_shapes=[pltpu.VMEM((tm, d), BF16)],
        compiler_params=_params("arbitrary", "arbitrary"),
        name="inproj",
    )(x2d, sc, sh, w, b, wg, bg)


def _conv_body(vc_ref, gc_ref, vh_ref, gh_ref, w_ref, b_ref, lg_ref, lb_ref, o_ref, u_scr, y_scr, *, rc):
    i = pl.program_id(1)
    ts = vc_ref.shape[1]
    ch = vc_ref.shape[2]
    uh = vh_ref[0].astype(F32) * _sigmoid(gh_ref[0].astype(F32))
    u_scr[0:CONV_HALO, :] = jnp.where(i > 0, uh, 0.0)

    def glu_chunk(r, carry):
        r0 = pl.multiple_of(r * rc, rc)
        v = vc_ref[0, pl.ds(r0, rc), :].astype(F32)
        g = gc_ref[0, pl.ds(r0, rc), :].astype(F32)
        u_scr[pl.ds(CONV_HALO + r0, rc), :] = v * _sigmoid(g)
        return carry

    lax.fori_loop(0, ts // rc, glu_chunk, 0)

    first_off = CONV_HALO - (CONV_WIDTH - 1)
    n_lane_chunks = ch // LANES

    def conv_chunk(idx, carry):
        r0 = pl.multiple_of((idx // n_lane_chunks) * rc, rc)
        l0 = pl.multiple_of((idx % n_lane_chunks) * LANES, LANES)
        lanes = pl.ds(l0, LANES)
        y = None
        for p in range(SUBLANES):
            win = rc if p == 0 else rc + SUBLANES
            v = None
            for a in range(CONV_HALO // SUBLANES + 1):
                off = SUBLANES * a + p
                if off < first_off or off > CONV_HALO:
                    continue
                tap = off - first_off
                term = w_ref[tap:tap + 1, lanes] * u_scr[pl.ds(r0 + SUBLANES * a, win), lanes]
                v = term if v is None else v + term
            vs = v if p == 0 else v[p:p + rc, :]
            y = vs if y is None else y + vs
        y_scr[pl.ds(r0, rc), lanes] = y + b_ref[:, lanes]
        return carry

    lax.fori_loop(0, (ts // rc) * n_lane_chunks, conv_chunk, 0)

    def ln_chunk(r, carry):
        r0 = pl.multiple_of(r * rc, rc)
        yn = _ln_rows(y_scr[pl.ds(r0, rc), :]) * lg_ref[...] + lb_ref[...]
        o_ref[0, pl.ds(r0, rc), :] = (yn * _sigmoid(yn)).astype(o_ref.dtype)
        return carry

    lax.fori_loop(0, ts // rc, ln_chunk, 0)


def _conv(z3, dw_w, dw_b, ln_g, ln_b, *, ts, ch):
    bsz, seq, _ = z3.shape
    halo_blocks = ts // CONV_HALO
    body = functools.partial(_conv_body, rc=min(ts, 64))
    row = lambda a: a.reshape(1, ch)
    return pl.pallas_call(
        body,
        grid=(bsz, seq // ts),
        in_specs=[pl.BlockSpec((1, ts, ch), lambda b, i: (b, i, 0)),
                  pl.BlockSpec((1, ts, ch), lambda b, i: (b, i, 1)),
                  pl.BlockSpec((1, CONV_HALO, ch), lambda b, i: (b, jnp.maximum(i * halo_blocks - 1, 0), 0)),
                  pl.BlockSpec((1, CONV_HALO, ch),
                               lambda b, i: (b, jnp.maximum(i * halo_blocks - 1, 0), ch // ch)),
                  pl.BlockSpec((CONV_WIDTH, ch), lambda b, i: (0, 0)),
                  pl.BlockSpec((1, ch), lambda b, i: (0, 0)),
                  pl.BlockSpec((1, ch), lambda b, i: (0, 0)),
                  pl.BlockSpec((1, ch), lambda b, i: (0, 0))],
        out_specs=pl.BlockSpec((1, ts, ch), lambda b, i: (b, i, 0)),
        out_shape=jax.ShapeDtypeStruct((bsz, seq, ch), BF16),
        scratch_shapes=[pltpu.VMEM((ts + CONV_HALO, ch), F32), pltpu.VMEM((ts, ch), F32)],
        compiler_params=_params("arbitrary", "arbitrary"),
        name="conv",
    )(z3, z3, z3, z3, dw_w, row(dw_b), row(ln_g), row(ln_b))


def _split3_dot(tri, x):
    hi = x.astype(BF16)
    r1 = x - hi.astype(F32)
    mid = r1.astype(BF16)
    lo = (r1 - mid.astype(F32)).astype(BF16)
    dot = lambda p: jnp.dot(tri, p, preferred_element_type=F32)
    return dot(hi) + dot(mid) + dot(lo)


def _mlstm_body(q_ref, k_ref, v_ref, o_ref, g_ref, mhg_ref, y_ref, c_scr, n_scr, m_scr, *, dk, dv):
    heads = MLSTM_HEADS
    ln = q_ref.shape[1]
    scale = dk ** -0.5

    @pl.when(pl.program_id(1) == 0)
    def _():
        c_scr[...] = jnp.zeros_like(c_scr)
        n_scr[...] = jnp.zeros_like(n_scr)
        m_scr[...] = jnp.zeros_like(m_scr)

    g = g_ref[0]
    lane = lax.broadcasted_iota(I32, g.shape, 1)
    lf = jnp.minimum(g, 0.0) - jnp.log1p(jnp.exp(-jnp.abs(g)))
    row_i = lax.broadcasted_iota(I32, (ln, ln), 0)
    col_i = lax.broadcasted_iota(I32, (ln, ln), 1)
    causal = col_i <= row_i
    tri = jnp.where(causal, 1.0, 0.0).astype(BF16)
    b = _split3_dot(tri, lf)
    cmb_t = jnp.where(lane < heads, g, b).T

    for h in range(heads):
        ig_col = g[:, h:h + 1]
        b_col = b[:, heads + h:heads + h + 1]
        ig_row = cmb_t[h:h + 1, :]
        b_row = cmb_t[heads + h:heads + h + 1, :]
        m_prev = m_scr[h][0:1, 0:1]
        dmat = jnp.where(causal, b_col - b_row + ig_row, -jnp.inf)
        inter = b_col + m_prev
        m_q = jnp.maximum(inter, jnp.max(dmat, axis=-1, keepdims=True))
        p = jnp.exp(dmat - m_q)
        qh = q_ref[0, :, h * dk:(h + 1) * dk]
        kh = k_ref[0, :, h * dk:(h + 1) * dk]
        vh = v_ref[0, :, h * dv:(h + 1) * dv]
        qk = lax.dot_general(qh, kh, (((1,), (1,)), ((), ())), preferred_element_type=F32)
        s = qk * (p * scale)
        w_inter = jnp.exp(inter - m_q)
        c_h = c_scr[h]
        n_h = n_scr[h]
        qc = jnp.dot(qh, c_h.astype(BF16), preferred_element_type=F32)
        num = jnp.dot(s.astype(BF16), vh, preferred_element_type=F32) + w_inter * qc
        qn = jnp.sum(qh.astype(F32) * n_h, axis=-1, keepdims=True)
        den = jnp.sum(s, axis=-1, keepdims=True) + w_inter * qn
        hh = num / jnp.maximum(jnp.abs(den), jnp.exp(-m_q))
        ms = jnp.mean(hh * hh, axis=-1, keepdims=True)
        hn = hh * lax.rsqrt(ms + LN_EPS) * mhg_ref[:, h * dv:(h + 1) * dv]
        og = o_ref[0, :, h * dv:(h + 1) * dv].astype(F32)
        y_ref[0, :, h * dv:(h + 1) * dv] = (hn * _sigmoid(og)).astype(y_ref.dtype)

        b_last = b[ln - 1:ln, heads + h:heads + h + 1]
        gain = b_last - b_col + ig_col
        m_new = jnp.maximum(b_last + m_prev, jnp.max(gain, axis=0, keepdims=True))
        wk = jnp.exp(gain - m_new) * scale
        decay = jnp.exp(b_last + m_prev - m_new)
        kw = kh.astype(F32) * wk
        kv = lax.dot_general(kw.astype(BF16), vh, (((0,), (0,)), ((), ())), preferred_element_type=F32)
        c_scr[h] = decay * c_h + kv
        n_scr[h] = decay * n_h + jnp.sum(kw, axis=0, keepdims=True)
        m_scr[h] = jnp.broadcast_to(m_new, (SUBLANES, LANES))


def _mlstm(z3, gates3, mh_g, *, ln, dk, dv, qk_col, v_col, o_col):
    bsz, seq, _ = z3.shape
    heads = MLSTM_HEADS
    body = functools.partial(_mlstm_body, dk=dk, dv=dv)
    return pl.pallas_call(
        body,
        grid=(bsz, seq // ln),
        in_specs=[pl.BlockSpec((1, ln, heads * dk), lambda b, i: (b, i, 2 * qk_col)),
                  pl.BlockSpec((1, ln, heads * dk), lambda b, i: (b, i, 2 * qk_col + 1)),
                  pl.BlockSpec((1, ln, heads * dv), lambda b, i: (b, i, v_col)),
                  pl.BlockSpec((1, ln, heads * dv), lambda b, i: (b, i, o_col)),
                  pl.BlockSpec((1, ln, LANES), lambda b, i: (b, i, 0)),
                  pl.BlockSpec((1, heads * dv), lambda b, i: (0, 0))],
        out_specs=pl.BlockSpec((1, ln, heads * dv), lambda b, i: (b, i, 0)),
        out_shape=jax.ShapeDtypeStruct((bsz, seq, heads * dv), BF16),
        scratch_shapes=[pltpu.VMEM((heads, dk, dv), F32),
                        pltpu.VMEM((heads, 1, dk), F32),
                        pltpu.VMEM((heads, SUBLANES, LANES), F32)],
        compiler_params=_params("arbitrary", "arbitrary"),
        name="mlstm",
    )(z3, z3, z3, z3, gates3, mh_g.reshape(1, heads * dv))


def _lane_pick(cols):
    rows = cols[0].shape[0]
    lane = lax.broadcasted_iota(I32, (rows, LANES), 1)
    out = jnp.zeros((rows, LANES), cols[0].dtype)
    for k, col in enumerate(cols):
        out = jnp.where(lane == k, col, out)
    return out


def _outproj_body(yc_ref, ym_ref, x_ref, g1_ref, l1g_ref, l1b_ref, sc2_ref, sh2_ref, wo_ref, rwh_ref, rwl_ref,
                  rb_ref, x1_ref, h2_ref, idx_ref, tw_ref, pos_ref, cnt_ref, carry_scr, *, alpha):
    @pl.when(pl.program_id(0) == 0)
    def _():
        carry_scr[...] = jnp.zeros_like(carry_scr)

    tm = x_ref.shape[0]
    dc = yc_ref.shape[1]
    ymix = (jnp.dot(yc_ref[...], wo_ref[0:dc, :], preferred_element_type=F32)
            + jnp.dot(ym_ref[...], wo_ref[dc:, :], preferred_element_type=F32))
    x1 = _ln_rows(alpha * x_ref[...] + g1_ref[0] * ymix) * l1g_ref[...] + l1b_ref[...]
    x1_ref[...] = x1
    h2 = _ln_rows(x1) * (1.0 + sc2_ref[0]) + sh2_ref[0]
    h2_ref[...] = h2

    h2_hi = h2.astype(BF16)
    h2_lo = (h2 - h2_hi.astype(F32)).astype(BF16)
    rdot = lambda a, w: jnp.dot(a, w[...], preferred_element_type=F32)
    logits = rdot(h2_hi, rwh_ref) + rdot(h2_hi, rwl_ref) + rdot(h2_lo, rwh_ref) + rb_ref[...]

    lane_f = lax.broadcasted_iota(I32, (tm, LANES), 1).astype(F32)
    work = logits
    sel_idx, sel_val = [], []
    for _ in range(TOP_K):
        mx = jnp.max(work, axis=-1, keepdims=True)
        am = jnp.min(jnp.where(work == mx, lane_f, float(LANES)), axis=-1, keepdims=True)
        sel_idx.append(am)
        sel_val.append(mx)
        work = jnp.where(lane_f == am, NEG_BIG, work)
    ex = [jnp.exp(v - sel_val[0]) for v in sel_val]
    denom = ex[0] + ex[1] + ex[2] + ex[3]
    tw_ref[...] = _lane_pick([e / denom for e in ex])
    idx_ref[...] = _lane_pick([a.astype(I32) for a in sel_idx])

    onehot = jnp.zeros((tm, LANES), F32)
    for a in sel_idx:
        onehot = onehot + jnp.where(lane_f == a, 1.0, 0.0)
    row_i = lax.broadcasted_iota(I32, (tm, tm), 0)
    col_i = lax.broadcasted_iota(I32, (tm, tm), 1)
    strict = jnp.where(col_i < row_i, 1.0, 0.0).astype(BF16)
    before = jnp.dot(strict, onehot.astype(BF16), preferred_element_type=F32) + carry_scr[0:1, :]
    pos = [jnp.sum(jnp.where(lane_f == a, before, 0.0), axis=-1, keepdims=True).astype(I32) for a in sel_idx]
    pos_ref[...] = _lane_pick(pos)
    total = carry_scr[0:1, :] + jnp.sum(onehot, axis=0, keepdims=True)
    carry_scr[...] = jnp.broadcast_to(total, carry_scr.shape)
    cnt_ref[...] = jnp.broadcast_to(total, cnt_ref.shape)


def _outproj(yc, ym, x2d, g1, l1g, l1b, sc2, sh2, wo, rwh, rwl, rb, *, seq, tm, alpha):
    t, d = x2d.shape
    dc = yc.shape[1]
    per_batch = seq // tm
    body = functools.partial(_outproj_body, alpha=alpha)
    full = lambda shape: pl.BlockSpec(shape, lambda i: tuple(0 for _ in shape))
    per_b = pl.BlockSpec((1, 1, d), lambda i: (i // per_batch, 0, 0))
    rows = lambda w: pl.BlockSpec((tm, w), lambda i: (i, 0))
    return pl.pallas_call(
        body,
        grid=(t // tm,),
        in_specs=[rows(dc), rows(dc), rows(d), per_b, full((1, d)), full((1, d)), per_b, per_b,
                  full((d, d)), full((d, LANES)), full((d, LANES)), full((1, LANES))],
        out_specs=[rows(d), rows(d), rows(LANES), rows(LANES), rows(LANES), full((SUBLANES, LANES))],
        out_shape=[jax.ShapeDtypeStruct((t, d), F32), jax.ShapeDtypeStruct((t, d), F32),
                   jax.ShapeDtypeStruct((t, LANES), I32), jax.ShapeDtypeStruct((t, LANES), F32),
                   jax.ShapeDtypeStruct((t, LANES), I32), jax.ShapeDtypeStruct((SUBLANES, LANES), F32)],
        scratch_shapes=[pltpu.VMEM((SUBLANES, LANES), F32)],
        compiler_params=_params("arbitrary"),
        name="outproj",
    )(yc, ym, x2d, g1, l1g, l1b, sc2, sh2, wo, rwh, rwl, rb)


def _dest_body(idx_ref, pos_ref, start_ref, o_ref):
    rows = idx_ref.shape[0]
    lane = lax.broadcasted_iota(I32, (rows, LANES), 1)
    start = start_ref[...]
    cols = []
    for k in range(TOP_K):
        seg = jnp.sum(jnp.where(lane == idx_ref[:, k:k + 1], start, 0.0), axis=-1, keepdims=True)
        cols.append(pos_ref[:, k:k + 1] + seg.astype(I32))
    o_ref[...] = _lane_pick(cols)


def _dest(top_idx, pos, seg_start_f, *, tr):
    t = top_idx.shape[0]
    rows = pl.BlockSpec((tr, LANES), lambda i: (i, 0))
    return pl.pallas_call(
        _dest_body,
        grid=(t // tr,),
        in_specs=[rows, rows, pl.BlockSpec((1, LANES), lambda i: (0, 0))],
        out_specs=rows,
        out_shape=jax.ShapeDtypeStruct((t, LANES), I32),
        compiler_params=_params("arbitrary"),
        name="dest",
    )(top_idx, pos, seg_start_f)


def _dispatch_body(dest_ref, cnt_ref, start_ref, h2_ref, xs_ref, zero_scr, sem, zsem):
    tr = h2_ref.shape[0]

    @pl.when(pl.program_id(0) == 0)
    def _():
        zero_scr[...] = jnp.zeros_like(zero_scr)

        def zero_copy(row):
            return pltpu.make_async_copy(_row(zero_scr, 0), _row(xs_ref, row), zsem)

        def per_expert(e, carry):
            cnt = cnt_ref[e]
            n_pad = (-cnt) & (MOE_SUB - 1)
            base = start_ref[e] + cnt

            def z_issue(r, c):
                zero_copy(base + r).start()
                return c

            def z_drain(r, c):
                zero_copy(0).wait()
                return c

            lax.fori_loop(0, n_pad, z_issue, 0)
            lax.fori_loop(0, n_pad, z_drain, 0)
            return carry

        lax.fori_loop(0, N_EXPERTS, per_expert, 0)

    def issue(g, carry):
        for u in range(ISSUE_UNROLL):
            r = g * ISSUE_UNROLL + u
            for k in range(TOP_K):
                pltpu.make_async_copy(_row(h2_ref, r), _row(xs_ref, dest_ref[r * TOP_K + k]),
                                      sem).start(priority=k % 2)
        return carry

    lax.fori_loop(0, tr // ISSUE_UNROLL, issue, 0)
    n = tr * TOP_K
    pltpu.make_async_copy(xs_ref.at[pl.ds(0, n)], xs_ref.at[pl.ds(0, n)], sem).wait()


def _dispatch(dest_flat, counts, seg_start, h2, n_rows, *, tr):
    t, d = h2.shape
    return pl.pallas_call(
        _dispatch_body,
        grid=(t // tr,),
        in_specs=[pl.BlockSpec((tr * TOP_K,), lambda i: (i,), memory_space=pltpu.SMEM),
                  pl.BlockSpec(memory_space=pltpu.SMEM),
                  pl.BlockSpec(memory_space=pltpu.SMEM),
                  pl.BlockSpec((tr, d), lambda i: (i, 0))],
        out_specs=pl.BlockSpec(memory_space=pl.ANY),
        out_shape=jax.ShapeDtypeStruct((n_rows, d), F32),
        scratch_shapes=[pltpu.VMEM((SUBLANES, d), F32), pltpu.SemaphoreType.DMA(()),
                        pltpu.SemaphoreType.DMA(())],
        compiler_params=_params("arbitrary"),
        name="dispatch",
    )(dest_flat, counts, seg_start, h2)


def _moe_rows(rows, xs_ref, wg_ref, bg_ref, wu_ref, bu_ref, wd_ref, bd_ref, ys_ref, xb_scr):
    @pl.when(pl.program_id(1) == 0)
    def _():
        xb_scr[0:rows, :] = xs_ref[0:rows, :].astype(BF16)
        ys_ref[0:rows, :] = jnp.broadcast_to(bd_ref[0], (rows, ys_ref.shape[1]))

    xb = xb_scr[0:rows, :]
    a = jnp.dot(xb, wg_ref[0].astype(BF16), preferred_element_type=F32) + bg_ref[0]
    u = jnp.dot(xb, wu_ref[0].astype(BF16), preferred_element_type=F32) + bu_ref[0]
    a = jnp.minimum(a, SWIGLU_LIMIT)
    u = jnp.clip(u, -SWIGLU_LIMIT, SWIGLU_LIMIT)
    act = a * _sigmoid(SWIGLU_ALPHA * a) * (u + 1.0)
    ys_ref[0:rows, :] += jnp.dot(act.astype(BF16), wd_ref[0].astype(BF16), preferred_element_type=F32)


def _moe_body(be_ref, bs_ref, ns_ref, xs_ref, wg_ref, bg_ref, wu_ref, bu_ref, wd_ref, bd_ref, ys_ref, xb_scr):
    del be_ref, bs_ref
    n_sub = ns_ref[pl.program_id(0)]
    for s in range(1, xs_ref.shape[0] // MOE_SUB + 1):
        @pl.when(n_sub == s)
        def _():
            _moe_rows(s * MOE_SUB, xs_ref, wg_ref, bg_ref, wu_ref, bu_ref, wd_ref, bd_ref, ys_ref, xb_scr)


def _moe(blk_e, blk_src, blk_nsub, xs, w_gate, b_gate, w_up, b_up, w_down, b_down, *, rb, tn):
    nr = xs.shape[0]
    e, d, de = w_gate.shape
    nj = de // tn

    def jj(i, j, ns):
        return jnp.where(ns[i] > 0, j, nj - 1)

    row_tiles = pl.BlockSpec((rb, d), lambda i, j, be, bs, ns: (bs[i], 0))
    grid_spec = pltpu.PrefetchScalarGridSpec(
        num_scalar_prefetch=3,
        grid=(nr // rb, nj),
        in_specs=[row_tiles,
                  pl.BlockSpec((1, d, tn), lambda i, j, be, bs, ns: (be[i], 0, jj(i, j, ns))),
                  pl.BlockSpec((1, 1, tn), lambda i, j, be, bs, ns: (be[i], 0, jj(i, j, ns))),
                  pl.BlockSpec((1, d, tn), lambda i, j, be, bs, ns: (be[i], 0, jj(i, j, ns))),
                  pl.BlockSpec((1, 1, tn), lambda i, j, be, bs, ns: (be[i], 0, jj(i, j, ns))),
                  pl.BlockSpec((1, tn, d), lambda i, j, be, bs, ns: (be[i], jj(i, j, ns), 0)),
                  pl.BlockSpec((1, 1, d), lambda i, j, be, bs, ns: (be[i], 0, 0))],
        out_specs=row_tiles,
        scratch_shapes=[pltpu.VMEM((rb, d), BF16)],
    )
    return pl.pallas_call(
        _moe_body,
        grid_spec=grid_spec,
        out_shape=jax.ShapeDtypeStruct((nr, d), F32),
        compiler_params=_params("arbitrary", "arbitrary"),
        name="moe",
    )(blk_e, blk_src, blk_nsub, xs, w_gate, b_gate.reshape(e, 1, de), w_up, b_up.reshape(e, 1, de),
      w_down, b_down.reshape(e, 1, d))


def _combine_body(dcur_ref, dnext_ref, tw_ref, x1_ref, g2_ref, lg_ref, lb_ref, ys_ref, o_ref,
                  ybuf, sems, *, alpha):
    i = pl.program_id(0)
    n = pl.num_programs(0)
    tc = x1_ref.shape[0]
    slot = i % 2

    def gather(dest_ref, buf_slot):
        def issue(g, carry):
            for u in range(ISSUE_UNROLL):
                r = g * ISSUE_UNROLL + u
                for k in range(TOP_K):
                    pltpu.make_async_copy(_row(ys_ref, dest_ref[r * TOP_K + k]), _row(ybuf.at[buf_slot, k], r),
                                          sems.at[buf_slot]).start(priority=k % 2)
            return carry

        lax.fori_loop(0, tc // ISSUE_UNROLL, issue, 0)

    @pl.when(i == 0)
    def _():
        gather(dcur_ref, 0)

    @pl.when(i + 1 < n)
    def _():
        gather(dnext_ref, 1 - slot)

    pltpu.make_async_copy(ybuf.at[slot], ybuf.at[slot], sems.at[slot]).wait()

    tw = tw_ref[...]
    y_moe = tw[:, 0:1] * ybuf[slot, 0]
    for k in range(1, TOP_K):
        y_moe = y_moe + tw[:, k:k + 1] * ybuf[slot, k]
    r = alpha * x1_ref[...] + g2_ref[0] * y_moe
    o_ref[...] = _ln_rows(r) * lg_ref[...] + lb_ref[...]


def _combine(dest_flat, tw, x1, g2, lg, lb, ys, *, seq, tc, alpha):
    t, d = x1.shape
    per_batch = seq // tc
    n_tiles = t // tc
    body = functools.partial(_combine_body, alpha=alpha)
    return pl.pallas_call(
        body,
        grid=(n_tiles,),
        in_specs=[pl.BlockSpec((tc * TOP_K,), lambda i: (i,), memory_space=pltpu.SMEM),
                  pl.BlockSpec((tc * TOP_K,), lambda i: (jnp.minimum(i + 1, n_tiles - 1),),
                               memory_space=pltpu.SMEM),
                  pl.BlockSpec((tc, LANES), lambda i: (i, 0)),
                  pl.BlockSpec((tc, d), lambda i: (i, 0)),
                  pl.BlockSpec((1, 1, d), lambda i: (i // per_batch, 0, 0)),
                  pl.BlockSpec((1, d), lambda i: (0, 0)),
                  pl.BlockSpec((1, d), lambda i: (0, 0)),
                  pl.BlockSpec(memory_space=pl.ANY)],
        out_specs=pl.BlockSpec((tc, d), lambda i: (i, 0)),
        out_shape=jax.ShapeDtypeStruct((t, d), F32),
        scratch_shapes=[pltpu.VMEM((2, TOP_K, tc, d), F32), pltpu.SemaphoreType.DMA((2,))],
        compiler_params=_params("arbitrary"),
        name="combine",
    )(dest_flat, dest_flat, tw, x1, g2, lg, lb, ys)


def _tiles(seq):
    return dict(
        ada_tn=1024,
        inproj_tm=min(seq, 1024), inproj_tn=512,
        conv_ts=min(seq, 512),
        mlstm_len=min(seq, 256),
        outproj_tm=min(seq, 512),
        dest_tr=min(seq, 1024), dispatch_tr=min(seq, 512),
        moe_rows=1024, moe_tn=256,
        combine_tc=min(seq, 256),
    )


def _layer(x, c, w_ada, b_ada, w_in, b_in, dw_w, dw_b, conv_ln_g, conv_ln_b, mh_g, w_out, ln1_g, ln1_b,
           router_w, router_b, w_gate, b_gate, w_up, b_up, w_down, b_down, ln2_g, ln2_b, *, alpha, tiles):
    bsz, seq, d = x.shape
    t = bsz * seq
    heads = MLSTM_HEADS
    d_conv = dw_w.shape[1]
    d_mlstm = mh_g.shape[0]
    dv = d_mlstm // heads
    dk = dv // 2
    n_main = 2 * d_conv + 2 * heads * dk + 2 * d_mlstm
    assert w_in.shape[1] == n_main + 2 * heads and d_conv == d_mlstm == 2 * heads * dk

    mod = _ada(c, w_ada, b_ada, tiles["ada_tn"])
    sh1, sc1, g1, sh2, sc2, g2 = [m.reshape(bsz, 1, d) for m in jnp.split(mod, 6, axis=-1)]

    x2d = x.reshape(t, d)
    wg = jnp.pad(w_in[:, n_main:], ((0, 0), (0, LANES - 2 * heads))).astype(BF16)
    bg = jnp.pad(b_in[n_main:], (0, LANES - 2 * heads)).reshape(1, LANES)
    z, gates = _inproj(x2d, sc1, sh1, w_in[:, :n_main].astype(BF16), b_in[:n_main].reshape(1, n_main), wg, bg,
                       seq=seq, tm=tiles["inproj_tm"], tn=tiles["inproj_tn"])
    z3 = z.reshape(bsz, seq, n_main)

    y_conv = _conv(z3, dw_w, dw_b, conv_ln_g, conv_ln_b, ts=tiles["conv_ts"], ch=d_conv)
    y_mlstm = _mlstm(z3, gates.reshape(bsz, seq, LANES), mh_g, ln=tiles["mlstm_len"], dk=dk, dv=dv,
                     qk_col=2, v_col=3, o_col=4)

    rw = jnp.pad(router_w, ((0, 0), (0, LANES - N_EXPERTS)))
    rw_hi = rw.astype(BF16)
    rw_lo = (rw - rw_hi.astype(F32)).astype(BF16)
    rb = jnp.pad(router_b, (0, LANES - N_EXPERTS), constant_values=NEG_BIG).reshape(1, LANES)
    x1, h2, top_idx, top_w, pos, cnt = _outproj(
        y_conv.reshape(t, d_conv), y_mlstm.reshape(t, d_mlstm), x2d, g1, ln1_g.reshape(1, d), ln1_b.reshape(1, d),
        sc2, sh2, w_out.astype(BF16), rw_hi, rw_lo, rb, seq=seq, tm=tiles["outproj_tm"], alpha=alpha)

    rb_rows = tiles["moe_rows"]
    n_blocks = -(-t * TOP_K // rb_rows) + N_EXPERTS
    counts = cnt[0, :N_EXPERTS].astype(I32)
    seg_blocks = (counts + rb_rows - 1) // rb_rows
    blocks_end = jnp.cumsum(seg_blocks)
    first_block = blocks_end - seg_blocks
    seg_start = first_block * rb_rows
    n_used = blocks_end[-1]
    blk = jnp.arange(n_blocks, dtype=I32)
    blk_src = jnp.minimum(blk, n_used - 1)
    blk_e = jnp.minimum(jnp.sum((blocks_end[None, :] <= blk_src[:, None]).astype(I32), axis=1), N_EXPERTS - 1)
    rows_left = jnp.clip(counts[blk_e] - (blk_src - first_block[blk_e]) * rb_rows, 0, rb_rows)
    blk_nsub = jnp.where(blk < n_used, (rows_left + MOE_SUB - 1) // MOE_SUB, 0).astype(I32)

    seg_start_f = jnp.pad(seg_start.astype(F32), (0, LANES - N_EXPERTS)).reshape(1, LANES)
    dest = _dest(top_idx, pos, seg_start_f, tr=tiles["dest_tr"])
    dest_flat = dest[:, :TOP_K].reshape(t * TOP_K)
    xs = _dispatch(dest_flat, counts, seg_start.astype(I32), h2, n_blocks * rb_rows, tr=tiles["dispatch_tr"])
    ys = _moe(blk_e.astype(I32), blk_src, blk_nsub, xs, w_gate, b_gate, w_up, b_up, w_down, b_down,
              rb=rb_rows, tn=tiles["moe_tn"])
    out = _combine(dest_flat, top_w, x1, g2, ln2_g.reshape(1, d), ln2_b.reshape(1, d), ys,
                   seq=seq, tc=tiles["combine_tc"], alpha=alpha)
    return out.reshape(bsz, seq, d)


def kernel(x, c, w_ada, b_ada, w_in, b_in, dw_w, dw_b, conv_ln_g, conv_ln_b, mh_g, w_out, ln1_g, ln1_b, router_w, router_b, w_gate, b_gate, w_up, b_up, w_down, b_down, ln2_g, ln2_b):
    depth = w_ada.shape[0]
    alpha = (2 * depth) ** 0.25
    tiles = _tiles(x.shape[1])
    for l in range(depth):
        x = _layer(x, c, w_ada[l], b_ada[l], w_in[l], b_in[l], dw_w[l], dw_b[l], conv_ln_g[l], conv_ln_b[l],
                   mh_g[l], w_out[l], ln1_g[l], ln1_b[l], router_w[l], router_b[l], w_gate[l], b_gate[l],
                   w_up[l], b_up[l], w_down[l], b_down[l], ln2_g[l], ln2_b[l], alpha=alpha, tiles=tiles)
    return x
```

```python
import functools

import jax
import jax.numpy as jnp
from jax import lax
from jax.experimental import pallas as pl
from jax.experimental.pallas import tpu as pltpu

F32 = jnp.float32
BF16 = jnp.bfloat16
I32 = jnp.int32

CONV_WIDTH = 31
MLSTM_HEADS = 4
N_EXPERTS = 32
TOP_K = 4
SWIGLU_ALPHA = 1.702
SWIGLU_LIMIT = 7.0
LN_EPS = 1e-5
LANES = 128
SUBLANES = 8
CONV_HALO = 32
MOE_SUB = 256
ISSUE_UNROLL = 8
ELEMWISE_ROWS = 128
NEG_BIG = -1e30
VMEM_LIMIT = 56 * 1024 * 1024
MOE_VMEM_LIMIT = 60 * 1024 * 1024


def _sigmoid(x):
    return 1.0 / (1.0 + jnp.exp(-x))


def _ln_rows(x):
    mu = jnp.mean(x, axis=-1, keepdims=True)
    xc = x - mu
    var = jnp.mean(xc * xc, axis=-1, keepdims=True)
    return xc * lax.rsqrt(var + LN_EPS)


def _row(ref, r):
    return ref.at[pl.ds(r, 1)]


def _params(*sem, vmem_limit=VMEM_LIMIT):
    return pltpu.CompilerParams(dimension_semantics=sem, vmem_limit_bytes=vmem_limit)


def _ada_body(c_ref, w_ref, b_ref, o_ref):
    c = c_ref[...]
    a = (c * _sigmoid(c)).astype(BF16)
    o_ref[...] = jnp.dot(a, w_ref[...].astype(BF16), preferred_element_type=F32) + b_ref[...]


def _ada(c, w, b, tn):
    bsz, d = c.shape
    n = w.shape[1]
    return pl.pallas_call(
        _ada_body,
        grid=(n // tn,),
        in_specs=[pl.BlockSpec((bsz, d), lambda j: (0, 0)),
                  pl.BlockSpec((d, tn), lambda j: (0, j)),
                  pl.BlockSpec((1, tn), lambda j: (0, j))],
        out_specs=pl.BlockSpec((bsz, tn), lambda j: (0, j)),
        out_shape=jax.ShapeDtypeStruct((bsz, n), F32),
        compiler_params=_params("arbitrary"),
        name="ada",
    )(c, w, b.reshape(1, n))


def _inproj_body(x_hbm, sc0_ref, sh0_ref, scn_ref, shn_ref, w_ref, b_ref, wg_ref, bg_ref, z_ref, g_ref,
                 xf_scr, h_scr, sems, *, n_tiles, nj, row_chunk):
    i = pl.program_id(0)
    j = pl.program_id(1)
    tm = xf_scr.shape[1]
    part = tm // nj
    cur = i % 2
    nxt = 1 - cur

    def x_copy(tile, slot):
        rows = pl.ds(pl.multiple_of(tile * tm, tm), tm)
        return pltpu.make_async_copy(x_hbm.at[rows], xf_scr.at[slot], sems.at[slot])

    @pl.when(jnp.logical_and(i == 0, j == 0))
    def _():
        x_copy(0, 0).start()
        x_copy(1, 1).start()
        x_copy(0, 0).wait()
        scale = 1.0 + sc0_ref[0]
        shift = sh0_ref[0]

        def chunk(r, carry):
            rows = pl.ds(pl.multiple_of(r * row_chunk, row_chunk), row_chunk)
            h_scr[0, rows, :] = (_ln_rows(xf_scr[0, rows, :]) * scale + shift).astype(BF16)
            return carry

        lax.fori_loop(0, tm // row_chunk, chunk, 0)

    @pl.when(jnp.logical_and(j == 0, i + 1 < n_tiles))
    def _():
        x_copy(i + 1, nxt).wait()

    @pl.when(jnp.logical_and(j == 0, i + 2 < n_tiles))
    def _():
        x_copy(i + 2, cur).start()

    rows = pl.ds(pl.multiple_of(j * part, part), part)
    h_next = _ln_rows(xf_scr[nxt, rows, :]) * (1.0 + scn_ref[0]) + shn_ref[0]
    h_scr[nxt, rows, :] = h_next.astype(BF16)
    h = h_scr[cur]
    z = jnp.dot(h, w_ref[...], preferred_element_type=F32) + b_ref[...]
    z_ref[...] = z.astype(z_ref.dtype)

    @pl.when(j == 0)
    def _():
        g_ref[...] = jnp.dot(h_scr[cur], wg_ref[...], preferred_element_type=F32) + bg_ref[...]


def _inproj(x2d, sc, sh, w, b, wg, bg, *, seq, tm, tn):
    t, d = x2d.shape
    n = w.shape[1]
    per_batch = seq // tm
    n_tiles = t // tm
    nj = n // tn
    assert n_tiles >= 2 and tm % nj == 0
    body = functools.partial(_inproj_body, n_tiles=n_tiles, nj=nj, row_chunk=min(tm, ELEMWISE_ROWS))
    first = pl.BlockSpec((1, 1, d), lambda i, j: (0, 0, 0))
    following = pl.BlockSpec((1, 1, d), lambda i, j: (jnp.minimum(i + 1, n_tiles - 1) // per_batch, 0, 0))
    return pl.pallas_call(
        body,
        grid=(n_tiles, nj),
        in_specs=[pl.BlockSpec(memory_space=pl.ANY), first, first, following, following,
                  pl.BlockSpec((d, tn), lambda i, j: (0, j)),
                  pl.BlockSpec((1, tn), lambda i, j: (0, j)),
                  pl.BlockSpec((d, LANES), lambda i, j: (0, 0)),
                  pl.BlockSpec((1, LANES), lambda i, j: (0, 0))],
        out_specs=[pl.BlockSpec((tm, tn), lambda i, j: (i, j)),
                   pl.BlockSpec((tm, LANES), lambda i, j: (i, 0))],
        out_shape=[jax.ShapeDtypeStruct((t, n), BF16), jax.ShapeDtypeStruct((t, LANES), F32)],
        scratch_shapes=[pltpu.VMEM((2, tm, d), F32), pltpu.VMEM((2, tm, d), BF16), pltpu.SemaphoreType.DMA((2,))],
        compiler_params=_params("arbitrary", "arbitrary"),
        name="inproj",
    )(x2d, sc, sh, sc, sh, w, b, wg, bg)


def _conv_body(vc_ref, gc_ref, vh_ref, gh_ref, w_ref, b_ref, lg_ref, lb_ref, o_ref, u_scr, y_scr, *, rc):
    i = pl.program_id(1)
    ts = vc_ref.shape[1]
    ch = vc_ref.shape[2]
    uh = vh_ref[0].astype(F32) * _sigmoid(gh_ref[0].astype(F32))
    u_scr[0:CONV_HALO, :] = jnp.where(i > 0, uh, 0.0)

    ec = min(ts, ELEMWISE_ROWS)

    def glu_chunk(r, carry):
        r0 = pl.multiple_of(r * ec, ec)
        v = vc_ref[0, pl.ds(r0, ec), :].astype(F32)
        g = gc_ref[0, pl.ds(r0, ec), :].astype(F32)
        u_scr[pl.ds(CONV_HALO + r0, ec), :] = v * _sigmoid(g)
        return carry

    lax.fori_loop(0, ts // ec, glu_chunk, 0)

    first_off = CONV_HALO - (CONV_WIDTH - 1)
    n_lane_chunks = ch // LANES

    def conv_chunk(idx, carry):
        r0 = pl.multiple_of((idx // n_lane_chunks) * rc, rc)
        l0 = pl.multiple_of((idx % n_lane_chunks) * LANES, LANES)
        lanes = pl.ds(l0, LANES)
        y = None
        for p in range(SUBLANES):
            win = rc if p == 0 else rc + SUBLANES
            v = None
            for a in range(CONV_HALO // SUBLANES + 1):
                off = SUBLANES * a + p
                if off < first_off or off > CONV_HALO:
                    continue
                tap = off - first_off
                term = w_ref[tap:tap + 1, lanes] * u_scr[pl.ds(r0 + SUBLANES * a, win), lanes]
                v = term if v is None else v + term
            vs = v if p == 0 else v[p:p + rc, :]
            y = vs if y is None else y + vs
        y_scr[pl.ds(r0, rc), lanes] = y + b_ref[:, lanes]
        return carry

    lax.fori_loop(0, (ts // rc) * n_lane_chunks, conv_chunk, 0)

    def ln_chunk(r, carry):
        r0 = pl.multiple_of(r * ec, ec)
        yn = _ln_rows(y_scr[pl.ds(r0, ec), :]) * lg_ref[...] + lb_ref[...]
        o_ref[0, pl.ds(r0, ec), :] = (yn * _sigmoid(yn)).astype(o_ref.dtype)
        return carry

    lax.fori_loop(0, ts // ec, ln_chunk, 0)


def _conv(z3, dw_w, dw_b, ln_g, ln_b, *, ts, ch):
    bsz, seq, _ = z3.shape
    halo_blocks = ts // CONV_HALO
    body = functools.partial(_conv_body, rc=min(ts, 128))
    row = lambda a: a.reshape(1, ch)
    return pl.pallas_call(
        body,
        grid=(bsz, seq // ts),
        in_specs=[pl.BlockSpec((1, ts, ch), lambda b, i: (b, i, 0)),
                  pl.BlockSpec((1, ts, ch), lambda b, i: (b, i, 1)),
                  pl.BlockSpec((1, CONV_HALO, ch), lambda b, i: (b, jnp.maximum(i * halo_blocks - 1, 0), 0)),
                  pl.BlockSpec((1, CONV_HALO, ch),
                               lambda b, i: (b, jnp.maximum(i * halo_blocks - 1, 0), ch // ch)),
                  pl.BlockSpec((CONV_WIDTH, ch), lambda b, i: (0, 0)),
                  pl.BlockSpec((1, ch), lambda b, i: (0, 0)),
                  pl.BlockSpec((1, ch), lambda b, i: (0, 0)),
                  pl.BlockSpec((1, ch), lambda b, i: (0, 0))],
        out_specs=pl.BlockSpec((1, ts, ch), lambda b, i: (b, i, 0)),
        out_shape=jax.ShapeDtypeStruct((bsz, seq, ch), BF16),
        scratch_shapes=[pltpu.VMEM((ts + CONV_HALO, ch), F32), pltpu.VMEM((ts, ch), F32)],
        compiler_params=_params("arbitrary", "arbitrary"),
        name="conv",
    )(z3, z3, z3, z3, dw_w, row(dw_b), row(ln_g), row(ln_b))


def _split3_dot(tri, x):
    hi = x.astype(BF16)
    r1 = x - hi.astype(F32)
    mid = r1.astype(BF16)
    lo = (r1 - mid.astype(F32)).astype(BF16)
    dot = lambda p: jnp.dot(tri, p, preferred_element_type=F32)
    return dot(hi) + dot(mid) + dot(lo)


def _mlstm_body(q_ref, k_ref, v_ref, o_ref, g_ref, mhg_ref, y_ref, c_scr, n_scr, m_scr, *, dk, dv):
    heads = MLSTM_HEADS
    ln = q_ref.shape[1]
    scale = dk ** -0.5

    @pl.when(pl.program_id(1) == 0)
    def _():
        c_scr[...] = jnp.zeros_like(c_scr)
        n_scr[...] = jnp.zeros_like(n_scr)
        m_scr[...] = jnp.zeros_like(m_scr)

    g = g_ref[0]
    lane = lax.broadcasted_iota(I32, g.shape, 1)
    lf = jnp.minimum(g, 0.0) - jnp.log1p(jnp.exp(-jnp.abs(g)))
    row_i = lax.broadcasted_iota(I32, (ln, ln), 0)
    col_i = lax.broadcasted_iota(I32, (ln, ln), 1)
    causal = col_i <= row_i
    tri = jnp.where(causal, 1.0, 0.0).astype(BF16)
    b = _split3_dot(tri, lf)
    cmb_t = jnp.where(lane < heads, g, b).T

    for h in range(heads):
        ig_col = g[:, h:h + 1]
        b_col = b[:, heads + h:heads + h + 1]
        ig_row = cmb_t[h:h + 1, :]
        b_row = cmb_t[heads + h:heads + h + 1, :]
        m_prev = m_scr[h][0:1, 0:1]
        dmat = jnp.where(causal, b_col - b_row + ig_row, -jnp.inf)
        inter = b_col + m_prev
        m_q = jnp.maximum(inter, jnp.max(dmat, axis=-1, keepdims=True))
        p = jnp.exp(dmat - m_q)
        qh = q_ref[0, :, h * dk:(h + 1) * dk]
        kh = k_ref[0, :, h * dk:(h + 1) * dk]
        vh = v_ref[0, :, h * dv:(h + 1) * dv]
        qk = lax.dot_general(qh, kh, (((1,), (1,)), ((), ())), preferred_element_type=F32)
        s = qk * (p * scale)
        w_inter = jnp.exp(inter - m_q)
        c_h = c_scr[h]
        n_h = n_scr[h]
        qc = jnp.dot(qh, c_h.astype(BF16), preferred_element_type=F32)
        num = jnp.dot(s.astype(BF16), vh, preferred_element_type=F32) + w_inter * qc
        qn = jnp.sum(qh.astype(F32) * n_h, axis=-1, keepdims=True)
        den = jnp.sum(s, axis=-1, keepdims=True) + w_inter * qn
        hh = num / jnp.maximum(jnp.abs(den), jnp.exp(-m_q))
        ms = jnp.mean(hh * hh, axis=-1, keepdims=True)
        hn = hh * lax.rsqrt(ms + LN_EPS) * mhg_ref[:, h * dv:(h + 1) * dv]
        og = o_ref[0, :, h * dv:(h + 1) * dv].astype(F32)
        y_ref[0, :, h * dv:(h + 1) * dv] = (hn * _sigmoid(og)).astype(y_ref.dtype)

        b_last = b[ln - 1:ln, heads + h:heads + h + 1]
        gain = b_last - b_col + ig_col
        m_new = jnp.maximum(b_last + m_prev, jnp.max(gain, axis=0, keepdims=True))
        wk = jnp.exp(gain - m_new) * scale
        decay = jnp.exp(b_last + m_prev - m_new)
        kw = kh.astype(F32) * wk
        kv = lax.dot_general(kw.astype(BF16), vh, (((0,), (0,)), ((), ())), preferred_element_type=F32)
        c_scr[h] = decay * c_h + kv
        n_scr[h] = decay * n_h + jnp.sum(kw, axis=0, keepdims=True)
        m_scr[h] = jnp.broadcast_to(m_new, (SUBLANES, LANES))


def _mlstm(z3, gates3, mh_g, *, ln, dk, dv, qk_col, v_col, o_col):
    bsz, seq, _ = z3.shape
    heads = MLSTM_HEADS
    body = functools.partial(_mlstm_body, dk=dk, dv=dv)
    return pl.pallas_call(
        body,
        grid=(bsz, seq // ln),
        in_specs=[pl.BlockSpec((1, ln, heads * dk), lambda b, i: (b, i, 2 * qk_col)),
                  pl.BlockSpec((1, ln, heads * dk), lambda b, i: (b, i, 2 * qk_col + 1)),
                  pl.BlockSpec((1, ln, heads * dv), lambda b, i: (b, i, v_col)),
                  pl.BlockSpec((1, ln, heads * dv), lambda b, i: (b, i, o_col)),
                  pl.BlockSpec((1, ln, LANES), lambda b, i: (b, i, 0)),
                  pl.BlockSpec((1, heads * dv), lambda b, i: (0, 0))],
        out_specs=pl.BlockSpec((1, ln, heads * dv), lambda b, i: (b, i, 0)),
        out_shape=jax.ShapeDtypeStruct((bsz, seq, heads * dv), BF16),
        scratch_shapes=[pltpu.VMEM((heads, dk, dv), F32),
                        pltpu.VMEM((heads, 1, dk), F32),
                        pltpu.VMEM((heads, SUBLANES, LANES), F32)],
        compiler_params=_params("arbitrary", "arbitrary"),
        name="mlstm",
    )(z3, z3, z3, z3, gates3, mh_g.reshape(1, heads * dv))


def _row_pick(rows):
    n = rows[0].shape[1]
    sub = lax.broadcasted_iota(I32, (SUBLANES, n), 0)
    out = jnp.zeros((SUBLANES, n), rows[0].dtype)
    for k, row in enumerate(rows):
        out = jnp.where(sub == k, row, out)
    return out


def _outproj_body(yc_ref, ym_ref, x_ref, g1_ref, l1g_ref, l1b_ref, sc2_ref, sh2_ref, wo_ref, rwh_ref, rwl_ref,
                  rb_ref, x1_ref, h2_ref, idx_ref, tw_ref, pos_ref, cnt_ref, carry_scr, *, alpha):
    @pl.when(pl.program_id(0) == 0)
    def _():
        carry_scr[...] = jnp.zeros_like(carry_scr)

    tm = x_ref.shape[0]
    dc = yc_ref.shape[1]
    n_exp = rwh_ref.shape[0]
    ymix = (jnp.dot(yc_ref[...], wo_ref[0:dc, :], preferred_element_type=F32)
            + jnp.dot(ym_ref[...], wo_ref[dc:, :], preferred_element_type=F32))
    x1 = _ln_rows(alpha * x_ref[...] + g1_ref[0] * ymix) * l1g_ref[...] + l1b_ref[...]
    x1_ref[...] = x1
    h2 = _ln_rows(x1) * (1.0 + sc2_ref[0]) + sh2_ref[0]
    h2_ref[...] = h2

    h2_hi = h2.astype(BF16)
    h2_lo = (h2 - h2_hi.astype(F32)).astype(BF16)
    tdot = lambda w, a: lax.dot_general(w[...], a, (((1,), (1,)), ((), ())), preferred_element_type=F32)
    logits = tdot(rwh_ref, h2_hi) + tdot(rwl_ref, h2_hi) + tdot(rwh_ref, h2_lo) + rb_ref[:, 0:1]

    ex_f = lax.broadcasted_iota(I32, (n_exp, tm), 0).astype(F32)
    work = logits
    sel_idx, sel_val = [], []
    for _ in range(TOP_K):
        mx = jnp.max(work, axis=0, keepdims=True)
        am = jnp.min(jnp.where(work == mx, ex_f, float(n_exp)), axis=0, keepdims=True)
        sel_idx.append(am)
        sel_val.append(mx)
        work = jnp.where(ex_f == am, NEG_BIG, work)
    ex = [jnp.exp(v - sel_val[0]) for v in sel_val]
    denom = ex[0] + ex[1] + ex[2] + ex[3]
    tw_ref[...] = _row_pick([e / denom for e in ex])
    idx_ref[...] = _row_pick([a.astype(I32) for a in sel_idx])

    onehot = jnp.zeros((n_exp, tm), F32)
    for a in sel_idx:
        onehot = onehot + jnp.where(ex_f == a, 1.0, 0.0)
    row_i = lax.broadcasted_iota(I32, (tm, tm), 0)
    col_i = lax.broadcasted_iota(I32, (tm, tm), 1)
    earlier = jnp.where(row_i < col_i, 1.0, 0.0).astype(BF16)
    carry = carry_scr[:, 0:1]
    before = jnp.dot(onehot.astype(BF16), earlier, preferred_element_type=F32) + carry
    pos = [jnp.sum(jnp.where(ex_f == a, before, 0.0), axis=0, keepdims=True).astype(I32) for a in sel_idx]
    pos_ref[...] = _row_pick(pos)
    total = carry + jnp.sum(onehot, axis=1, keepdims=True)
    carry_scr[...] = jnp.broadcast_to(total, carry_scr.shape)
    cnt_ref[...] = jnp.broadcast_to(total, cnt_ref.shape)


def _outproj(yc, ym, x2d, g1, l1g, l1b, sc2, sh2, wo, rwh, rwl, rb, *, seq, tm, alpha):
    t, d = x2d.shape
    dc = yc.shape[1]
    per_batch = seq // tm
    body = functools.partial(_outproj_body, alpha=alpha)
    full = lambda shape: pl.BlockSpec(shape, lambda i: tuple(0 for _ in shape))
    per_b = pl.BlockSpec((1, 1, d), lambda i: (i // per_batch, 0, 0))
    rows = lambda w: pl.BlockSpec((tm, w), lambda i: (i, 0))
    per_tok = pl.BlockSpec((SUBLANES, tm), lambda i: (0, i))
    n_exp = rwh.shape[0]
    return pl.pallas_call(
        body,
        grid=(t // tm,),
        in_specs=[rows(dc), rows(dc), rows(d), per_b, full((1, d)), full((1, d)), per_b, per_b,
                  full((d, d)), full((n_exp, d)), full((n_exp, d)), full((n_exp, LANES))],
        out_specs=[rows(d), rows(d), per_tok, per_tok, per_tok, full((n_exp, LANES))],
        out_shape=[jax.ShapeDtypeStruct((t, d), F32), jax.ShapeDtypeStruct((t, d), F32),
                   jax.ShapeDtypeStruct((SUBLANES, t), I32), jax.ShapeDtypeStruct((SUBLANES, t), F32),
                   jax.ShapeDtypeStruct((SUBLANES, t), I32), jax.ShapeDtypeStruct((n_exp, LANES), F32)],
        scratch_shapes=[pltpu.VMEM((n_exp, LANES), F32)],
        compiler_params=_params("arbitrary"),
        name="outproj",
    )(yc, ym, x2d, g1, l1g, l1b, sc2, sh2, wo, rwh, rwl, rb)


def _dest_body(idx_ref, pos_ref, start_ref, o_ref):
    n_exp = start_ref.shape[0]
    n = idx_ref.shape[1]
    ex = lax.broadcasted_iota(I32, (n_exp, n), 0)
    start = start_ref[:, 0:1]
    rows = []
    for k in range(TOP_K):
        seg = jnp.sum(jnp.where(ex == idx_ref[k:k + 1, :], start, 0.0), axis=0, keepdims=True)
        rows.append(pos_ref[k:k + 1, :] + seg.astype(I32))
    o_ref[...] = _row_pick(rows)


def _dest(top_idx, pos, seg_start_f, *, tr):
    t = top_idx.shape[1]
    per_tok = pl.BlockSpec((SUBLANES, tr), lambda i: (0, i))
    return pl.pallas_call(
        _dest_body,
        grid=(t // tr,),
        in_specs=[per_tok, per_tok, pl.BlockSpec(seg_start_f.shape, lambda i: (0, 0))],
        out_specs=per_tok,
        out_shape=jax.ShapeDtypeStruct((SUBLANES, t), I32),
        compiler_params=_params("arbitrary"),
        name="dest",
    )(top_idx, pos, seg_start_f)


def _dispatch_body(dest_ref, cnt_ref, start_ref, h2_ref, xs_ref, zero_scr, sem, zsem):
    tr = h2_ref.shape[0]

    @pl.when(pl.program_id(0) == 0)
    def _():
        zero_scr[...] = jnp.zeros_like(zero_scr)

        def zero_copy(row):
            return pltpu.make_async_copy(_row(zero_scr, 0), _row(xs_ref, row), zsem)

        def per_expert(e, carry):
            cnt = cnt_ref[e]
            n_pad = (-cnt) & (MOE_SUB - 1)
            base = start_ref[e] + cnt

            def z_issue(r, c):
                zero_copy(base + r).start()
                return c

            def z_drain(r, c):
                zero_copy(0).wait()
                return c

            lax.fori_loop(0, n_pad, z_issue, 0)
            lax.fori_loop(0, n_pad, z_drain, 0)
            return carry

        lax.fori_loop(0, N_EXPERTS, per_expert, 0)

    def issue(g, carry):
        for u in range(ISSUE_UNROLL):
            r = g * ISSUE_UNROLL + u
            for k in range(TOP_K):
                pltpu.make_async_copy(_row(h2_ref, r), _row(xs_ref, dest_ref[r * TOP_K + k]),
                                      sem).start(priority=k % 2)
        return carry

    lax.fori_loop(0, tr // ISSUE_UNROLL, issue, 0)
    n = tr * TOP_K
    pltpu.make_async_copy(xs_ref.at[pl.ds(0, n)], xs_ref.at[pl.ds(0, n)], sem).wait()


def _dispatch(dest_flat, counts, seg_start, h2, n_rows, *, tr):
    t, d = h2.shape
    return pl.pallas_call(
        _dispatch_body,
        grid=(t // tr,),
        in_specs=[pl.BlockSpec((tr * TOP_K,), lambda i: (i,), memory_space=pltpu.SMEM),
                  pl.BlockSpec(memory_space=pltpu.SMEM),
                  pl.BlockSpec(memory_space=pltpu.SMEM),
                  pl.BlockSpec((tr, d), lambda i: (i, 0))],
        out_specs=pl.BlockSpec(memory_space=pl.ANY),
        out_shape=jax.ShapeDtypeStruct((n_rows, d), F32),
        scratch_shapes=[pltpu.VMEM((SUBLANES, d), F32), pltpu.SemaphoreType.DMA(()),
                        pltpu.SemaphoreType.DMA(())],
        compiler_params=_params("arbitrary"),
        name="dispatch",
    )(dest_flat, counts, seg_start, h2)


def _moe_rows(rows, xf_scr, wg_ref, bg_ref, wu_ref, bu_ref, wd_ref, bd_ref, ys_ref, xb_scr):
    @pl.when(pl.program_id(1) == 0)
    def _():
        xb_scr[0:rows, :] = xf_scr[0:rows, :].astype(BF16)
        ys_ref[0:rows, :] = jnp.broadcast_to(bd_ref[0], (rows, ys_ref.shape[1]))

    xb = xb_scr[0:rows, :]
    a = jnp.dot(xb, wg_ref[0].astype(BF16), preferred_element_type=F32) + bg_ref[0]
    u = jnp.dot(xb, wu_ref[0].astype(BF16), preferred_element_type=F32) + bu_ref[0]
    a = jnp.minimum(a, SWIGLU_LIMIT)
    u = jnp.clip(u, -SWIGLU_LIMIT, SWIGLU_LIMIT)
    act = a * _sigmoid(SWIGLU_ALPHA * a) * (u + 1.0)
    ys_ref[0:rows, :] += jnp.dot(act.astype(BF16), wd_ref[0].astype(BF16), preferred_element_type=F32)


def _moe_body(be_ref, bs_ref, ns_ref, xs_ref, wg_ref, bg_ref, wu_ref, bu_ref, wd_ref, bd_ref, ys_ref,
              xf_scr, xb_scr, sem):
    del be_ref
    i = pl.program_id(0)
    j = pl.program_id(1)
    n_blocks = pl.num_programs(0)
    rb = xf_scr.shape[0]
    n_sub = ns_ref[i]

    def block_copies(blk, start):
        for s in range(rb // MOE_SUB):
            @pl.when(s < ns_ref[blk])
            def _():
                src = xs_ref.at[pl.ds(pl.multiple_of(bs_ref[blk] * rb + s * MOE_SUB, MOE_SUB), MOE_SUB)]
                cp = pltpu.make_async_copy(src, xf_scr.at[pl.ds(s * MOE_SUB, MOE_SUB)], sem)
                if start:
                    cp.start()
                else:
                    cp.wait()

    @pl.when(jnp.logical_and(i == 0, j == 0))
    def _():
        block_copies(0, True)

    @pl.when(j == 0)
    def _():
        block_copies(i, False)

    @pl.when(jnp.logical_and(j == 1, i + 1 < n_blocks))
    def _():
        block_copies(jnp.minimum(i + 1, n_blocks - 1), True)

    for s in range(1, rb // MOE_SUB + 1):
        @pl.when(n_sub == s)
        def _():
            _moe_rows(s * MOE_SUB, xf_scr, wg_ref, bg_ref, wu_ref, bu_ref, wd_ref, bd_ref, ys_ref, xb_scr)


def _moe(blk_e, blk_src, blk_nsub, xs, w_gate, b_gate, w_up, b_up, w_down, b_down, *, rb, tn):
    nr = xs.shape[0]
    e, d, de = w_gate.shape
    nj = de // tn
    assert nj >= 2

    def jj(i, j, ns):
        return jnp.where(ns[i] > 0, j, nj - 1)

    row_tiles = pl.BlockSpec((rb, d), lambda i, j, be, bs, ns: (bs[i], 0))
    grid_spec = pltpu.PrefetchScalarGridSpec(
        num_scalar_prefetch=3,
        grid=(nr // rb, nj),
        in_specs=[pl.BlockSpec(memory_space=pl.ANY),
                  pl.BlockSpec((1, d, tn), lambda i, j, be, bs, ns: (be[i], 0, jj(i, j, ns))),
                  pl.BlockSpec((1, 1, tn), lambda i, j, be, bs, ns: (be[i], 0, jj(i, j, ns))),
                  pl.BlockSpec((1, d, tn), lambda i, j, be, bs, ns: (be[i], 0, jj(i, j, ns))),
                  pl.BlockSpec((1, 1, tn), lambda i, j, be, bs, ns: (be[i], 0, jj(i, j, ns))),
                  pl.BlockSpec((1, tn, d), lambda i, j, be, bs, ns: (be[i], jj(i, j, ns), 0)),
                  pl.BlockSpec((1, 1, d), lambda i, j, be, bs, ns: (be[i], 0, 0))],
        out_specs=row_tiles,
        scratch_shapes=[pltpu.VMEM((rb, d), F32), pltpu.VMEM((rb, d), BF16), pltpu.SemaphoreType.DMA(())],
    )
    return pl.pallas_call(
        _moe_body,
        grid_spec=grid_spec,
        out_shape=jax.ShapeDtypeStruct((nr, d), F32),
        compiler_params=_params("arbitrary", "arbitrary", vmem_limit=MOE_VMEM_LIMIT),
        name="moe",
    )(blk_e, blk_src, blk_nsub, xs, w_gate, b_gate.reshape(e, 1, de), w_up, b_up.reshape(e, 1, de),
      w_down, b_down.reshape(e, 1, d))


def _combine_body(dcur_ref, dnext_ref, tw_ref, x1_ref, g2_ref, lg_ref, lb_ref, ys_ref, o_ref,
                  ybuf, sems, *, alpha, row_chunk):
    i = pl.program_id(0)
    n = pl.num_programs(0)
    tc = x1_ref.shape[0]
    slot = i % 2

    def gather(dest_ref, buf_slot):
        def issue(g, carry):
            for u in range(ISSUE_UNROLL):
                r = g * ISSUE_UNROLL + u
                for k in range(TOP_K):
                    pltpu.make_async_copy(_row(ys_ref, dest_ref[r * TOP_K + k]), _row(ybuf.at[buf_slot, k], r),
                                          sems.at[buf_slot]).start(priority=k % 2)
            return carry

        lax.fori_loop(0, tc // ISSUE_UNROLL, issue, 0)

    @pl.when(i == 0)
    def _():
        gather(dcur_ref, 0)

    @pl.when(i + 1 < n)
    def _():
        gather(dnext_ref, 1 - slot)

    pltpu.make_async_copy(ybuf.at[slot], ybuf.at[slot], sems.at[slot]).wait()

    g2 = g2_ref[0]

    def reduce_chunk(c, carry):
        rows = pl.ds(pl.multiple_of(c * row_chunk, row_chunk), row_chunk)
        tw = tw_ref[rows, :]
        y_moe = tw[:, 0:1] * ybuf[slot, 0, rows, :]
        for k in range(1, TOP_K):
            y_moe = y_moe + tw[:, k:k + 1] * ybuf[slot, k, rows, :]
        r = alpha * x1_ref[rows, :] + g2 * y_moe
        o_ref[rows, :] = _ln_rows(r) * lg_ref[...] + lb_ref[...]
        return carry

    lax.fori_loop(0, tc // row_chunk, reduce_chunk, 0)


def _combine(dest_flat, tw, x1, g2, lg, lb, ys, *, seq, tc, alpha):
    t, d = x1.shape
    per_batch = seq // tc
    n_tiles = t // tc
    body = functools.partial(_combine_body, alpha=alpha, row_chunk=min(tc, ELEMWISE_ROWS))
    return pl.pallas_call(
        body,
        grid=(n_tiles,),
        in_specs=[pl.BlockSpec((tc * TOP_K,), lambda i: (i,), memory_space=pltpu.SMEM),
                  pl.BlockSpec((tc * TOP_K,), lambda i: (jnp.minimum(i + 1, n_tiles - 1),),
                               memory_space=pltpu.SMEM),
                  pl.BlockSpec((tc, LANES), lambda i: (i, 0)),
                  pl.BlockSpec((tc, d), lambda i: (i, 0)),
                  pl.BlockSpec((1, 1, d), lambda i: (i // per_batch, 0, 0)),
                  pl.BlockSpec((1, d), lambda i: (0, 0)),
                  pl.BlockSpec((1, d), lambda i: (0, 0)),
                  pl.BlockSpec(memory_space=pl.ANY)],
        out_specs=pl.BlockSpec((tc, d), lambda i: (i, 0)),
        out_shape=jax.ShapeDtypeStruct((t, d), F32),
        scratch_shapes=[pltpu.VMEM((2, TOP_K, tc, d), F32), pltpu.SemaphoreType.DMA((2,))],
        compiler_params=_params("arbitrary"),
        name="combine",
    )(dest_flat, dest_flat, tw, x1, g2, lg, lb, ys)


def _tiles(seq):
    return dict(
        ada_tn=1024,
        inproj_tm=min(seq, 1024), inproj_tn=1280,
        conv_ts=min(seq, 1024),
        mlstm_len=min(seq, 256),
        outproj_tm=min(seq, 512),
        dest_tr=min(seq, 1024), dispatch_tr=min(seq, 512),
        moe_rows=1024, moe_tn=512,
        combine_tc=min(seq, 256),
    )


def _layer(x, c, w_ada, b_ada, w_in, b_in, dw_w, dw_b, conv_ln_g, conv_ln_b, mh_g, w_out, ln1_g, ln1_b,
           router_w, router_b, w_gate, b_gate, w_up, b_up, w_down, b_down, ln2_g, ln2_b, *, alpha, tiles):
    bsz, seq, d = x.shape
    t = bsz * seq
    heads = MLSTM_HEADS
    d_conv = dw_w.shape[1]
    d_mlstm = mh_g.shape[0]
    dv = d_mlstm // heads
    dk = dv // 2
    n_main = 2 * d_conv + 2 * heads * dk + 2 * d_mlstm
    assert w_in.shape[1] == n_main + 2 * heads and d_conv == d_mlstm == 2 * heads * dk

    mod = _ada(c, w_ada, b_ada, tiles["ada_tn"])
    sh1, sc1, g1, sh2, sc2, g2 = [m.reshape(bsz, 1, d) for m in jnp.split(mod, 6, axis=-1)]

    x2d = x.reshape(t, d)
    wg = jnp.pad(w_in[:, n_main:], ((0, 0), (0, LANES - 2 * heads))).astype(BF16)
    bg = jnp.pad(b_in[n_main:], (0, LANES - 2 * heads)).reshape(1, LANES)
    z, gates = _inproj(x2d, sc1, sh1, w_in[:, :n_main].astype(BF16), b_in[:n_main].reshape(1, n_main), wg, bg,
                       seq=seq, tm=tiles["inproj_tm"], tn=tiles["inproj_tn"])
    z3 = z.reshape(bsz, seq, n_main)

    y_conv = _conv(z3, dw_w, dw_b, conv_ln_g, conv_ln_b, ts=tiles["conv_ts"], ch=d_conv)
    y_mlstm = _mlstm(z3, gates.reshape(bsz, seq, LANES), mh_g, ln=tiles["mlstm_len"], dk=dk, dv=dv,
                     qk_col=2, v_col=3, o_col=4)

    rw = router_w.T
    rw_hi = rw.astype(BF16)
    rw_lo = (rw - rw_hi.astype(F32)).astype(BF16)
    rb = jnp.broadcast_to(router_b.reshape(N_EXPERTS, 1), (N_EXPERTS, LANES))
    x1, h2, top_idx, top_w, pos, cnt = _outproj(
        y_conv.reshape(t, d_conv), y_mlstm.reshape(t, d_mlstm), x2d, g1, ln1_g.reshape(1, d), ln1_b.reshape(1, d),
        sc2, sh2, w_out.astype(BF16), rw_hi, rw_lo, rb, seq=seq, tm=tiles["outproj_tm"], alpha=alpha)

    rb_rows = tiles["moe_rows"]
    n_blocks = -(-t * TOP_K // rb_rows) + N_EXPERTS
    counts = cnt[:, 0].astype(I32)
    seg_blocks = (counts + rb_rows - 1) // rb_rows
    blocks_end = jnp.cumsum(seg_blocks)
    first_block = blocks_end - seg_blocks
    seg_start = first_block * rb_rows
    n_used = blocks_end[-1]
    blk = jnp.arange(n_blocks, dtype=I32)
    blk_src = jnp.minimum(blk, n_used - 1)
    blk_e = jnp.minimum(jnp.sum((blocks_end[None, :] <= blk_src[:, None]).astype(I32), axis=1), N_EXPERTS - 1)
    rows_left = jnp.clip(counts[blk_e] - (blk_src - first_block[blk_e]) * rb_rows, 0, rb_rows)
    blk_nsub = jnp.where(blk < n_used, (rows_left + MOE_SUB - 1) // MOE_SUB, 0).astype(I32)

    seg_start_f = jnp.broadcast_to(seg_start.astype(F32).reshape(N_EXPERTS, 1), (N_EXPERTS, LANES))
    dest = _dest(top_idx, pos, seg_start_f, tr=tiles["dest_tr"])
    dest_flat = dest[:TOP_K].T.reshape(t * TOP_K)
    tw = jnp.pad(top_w[:TOP_K].T, ((0, 0), (0, LANES - TOP_K)))
    xs = _dispatch(dest_flat, counts, seg_start.astype(I32), h2, n_blocks * rb_rows, tr=tiles["dispatch_tr"])
    ys = _moe(blk_e.astype(I32), blk_src, blk_nsub, xs, w_gate, b_gate, w_up, b_up, w_down, b_down,
              rb=rb_rows, tn=tiles["moe_tn"])
    out = _combine(dest_flat, tw, x1, g2, ln2_g.reshape(1, d), ln2_b.reshape(1, d), ys,
                   seq=seq, tc=tiles["combine_tc"], alpha=alpha)
    return out.reshape(bsz, seq, d)


def kernel(x, c, w_ada, b_ada, w_in, b_in, dw_w, dw_b, conv_ln_g, conv_ln_b, mh_g, w_out, ln1_g, ln1_b, router_w, router_b, w_gate, b_gate, w_up, b_up, w_down, b_down, ln2_g, ln2_b):
    depth = w_ada.shape[0]
    alpha = (2 * depth) ** 0.25
    tiles = _tiles(x.shape[1])
    for l in range(depth):
        x = _layer(x, c, w_ada[l], b_ada[l], w_in[l], b_in[l], dw_w[l], dw_b[l], conv_ln_g[l], conv_ln_b[l],
                   mh_g[l], w_out[l], ln1_g[l], ln1_b[l], router_w[l], router_b[l], w_gate[l], b_gate[l],
                   w_up[l], b_up[l], w_down[l], b_down[l], ln2_g[l], ln2_b[l], alpha=alpha, tiles=tiles)
    return x
```
